```python
import math
import jax
import jax.numpy as jnp
from jax import lax
import numpy as np

D_MODEL = 1024
BATCH = 16
SEQ = 256
DEPTH = 2
DEC_BATCH = 4
DEC_SEQ = 4096
PAST_LEN = 512

GRID_W = 64
N_HEADS = 16
HEAD_DIM = D_MODEL // N_HEADS
D_RNN = D_MODEL // 2
RNN_HEADS = 8
RNN_BLOCK = D_RNN // RNN_HEADS
CONV_W = 4
D_FOURIER = D_MODEL // 2
FOURIER_GROUPS = 4
FOURIER_CH = D_FOURIER // FOURIER_GROUPS
D_IN_EVEN = 2 * D_RNN + D_FOURIER
D_OUT_EVEN = D_RNN + D_FOURIER
D_FF = 4 * D_MODEL
NA_ROWS = 8
NA_COLS = 16
C_SCALE = 8.0
N_EVEN = (DEPTH + 1) // 2
N_ODD = DEPTH // 2
ALPHA = (2 * DEPTH) ** 0.25
BETA = (8 * DEPTH) ** -0.25
LN_EPS = 1e-5

kernel_name = "hybrid_diffusion_rglru_fnet_natten_step"


def layer_norm(x, g=None, b=None):
    xf = x.astype(jnp.float32)
    mu = jnp.mean(xf, axis=-1, keepdims=True)
    var = jnp.mean(jnp.square(xf - mu), axis=-1, keepdims=True)
    y = (xf - mu) * lax.rsqrt(var + LN_EPS)
    if g is not None:
        y = y * g.astype(jnp.float32) + b.astype(jnp.float32)
    return y.astype(x.dtype)


def ada_params(cond, w, b):
    m = jax.nn.silu(cond) @ w + b
    return jnp.split(m[..., None, :], 6, axis=-1)


def modulate(x, shift, scale):
    return layer_norm(x) * (1.0 + scale) + shift


def depthwise_conv(x, w, b):
    S = x.shape[1]
    left = CONV_W // 2
    xp = jnp.pad(x, ((0, 0), (left, CONV_W - 1 - left), (0, 0)))
    out = b
    for k in range(CONV_W):
        out = out + xp[:, k:k + S] * w[k]
    return out


def block_diag(x, w, b):
    xb = x.reshape(x.shape[:-1] + (RNN_HEADS, RNN_BLOCK))
    y = jnp.einsum('bshi,hij->bshj', xb, w.astype(x.dtype))
    return y.reshape(x.shape) + b.astype(x.dtype)


def _lin_combine(e1, e2):
    a1, b1 = e1
    a2, b2 = e2
    return a1 * a2, a2 * b1 + b2


def rglru(xc, w_r, b_r, w_i, b_i, lam, h0, reverse):
    xf = xc.astype(jnp.float32)
    r = jax.nn.sigmoid(block_diag(xf, w_r, b_r))
    i = jax.nn.sigmoid(block_diag(xf, w_i, b_i))
    log_a = -C_SCALE * r * jax.nn.softplus(-lam.astype(jnp.float32))
    a = jnp.exp(log_a)
    bterm = jnp.sqrt(-jnp.expm1(2.0 * log_a)) * (i * xf)
    h0f = h0.astype(jnp.float32)
    if reverse:
        bterm = bterm.at[:, -1].add(a[:, -1] * h0f)
    else:
        bterm = bterm.at[:, 0].add(a[:, 0] * h0f)
    _, h = lax.associative_scan(_lin_combine, (a, bterm), reverse=reverse, axis=1)
    return h.astype(xc.dtype)


def fourier_mix(xf):
    B, S, _ = xf.shape
    z = xf.reshape(B, S, FOURIER_GROUPS, FOURIER_CH).astype(jnp.float32)
    z = jnp.fft.fft2(z, axes=(1, 3), norm='ortho').real
    return z.reshape(B, S, D_FOURIER).astype(xf.dtype)


def mix_even(h, h0, w_in, b_in, conv_w, conv_b, w_r, b_r, w_i, b_i, lam, w_out, b_out):
    u = h @ w_in + b_in
    x_rnn = u[..., :D_RNN]
    x_gate = u[..., D_RNN:2 * D_RNN]
    x_four = u[..., 2 * D_RNN:]
    xc = depthwise_conv(x_rnn, conv_w, conv_b)
    h_fwd = rglru(xc, w_r[0], b_r[0], w_i[0], b_i[0], lam[0], h0[:, 0], reverse=False)
    h_bwd = rglru(xc, w_r[1], b_r[1], w_i[1], b_i[1], lam[1], h0[:, 1], reverse=True)
    y_a = (h_fwd + h_bwd) * jax.nn.gelu(x_gate)
    y_b = fourier_mix(x_four)
    y = jnp.concatenate([y_a, y_b], axis=-1) @ w_out + b_out
    final_state = jnp.stack([h_fwd[:, -1], h_bwd[:, 0]], axis=1)
    return y, final_state


def split_heads(t):
    B, S, _ = t.shape
    return t.reshape(B, S, N_HEADS, HEAD_DIM).transpose(0, 2, 1, 3)


def attn_context(h, w_qkv, b_qkv, w_out, b_out):
    B, S, _ = h.shape
    qkv = h @ w_qkv + b_qkv
    q, k, v = (split_heads(t) for t in jnp.split(qkv, 3, axis=-1))
    s = jnp.einsum('bhqd,bhkd->bhqk', q, k).astype(jnp.float32) * (HEAD_DIM ** -0.5)
    p = jax.nn.softmax(s, axis=-1).astype(v.dtype)
    o = jnp.einsum('bhqk,bhkd->bhqd', p, v)
    o = o.transpose(0, 2, 1, 3).reshape(B, S, D_MODEL)
    return o @ w_out + b_out, k, v


def attn_latent(h, k_ctx, v_ctx, w_qkv, b_qkv, rpb, w_out, b_out):
    B, N, _ = h.shape
    rows = N // GRID_W
    kr = min(NA_ROWS, rows)
    qkv = h @ w_qkv + b_qkv
    q, k, v = (split_heads(t).reshape(B, N_HEADS, rows, GRID_W, HEAD_DIM)
               for t in jnp.split(qkv, 3, axis=-1))
    col = jnp.arange(GRID_W)
    col_start = jnp.clip(col - NA_COLS // 2, 0, GRID_W - NA_COLS)
    col_mask = (col[None, :] >= col_start[:, None]) & (col[None, :] < col_start[:, None] + NA_COLS)
    col_off = jnp.clip(col[None, :] - col[:, None] + NA_COLS - 1, 0, 2 * NA_COLS - 2)
    scale = HEAD_DIM ** -0.5
    s_ctx_all = None

    def row_block(i):
        rs = jnp.clip(i - kr // 2, 0, rows - kr)
        q_i = lax.dynamic_index_in_dim(q, i, axis=2, keepdims=False)
        k_blk = lax.dynamic_slice_in_dim(k, rs, kr, axis=2)
        v_blk = lax.dynamic_slice_in_dim(v, rs, kr, axis=2)
        s_loc = jnp.einsum('bhqd,bhakd->bhqak', q_i, k_blk).astype(jnp.float32) * scale
        row_off = rs + jnp.arange(kr) - i + NA_ROWS - 1
        bias = rpb[:, row_off][:, :, col_off].astype(jnp.float32)
        bias = bias.transpose(0, 2, 1, 3)
        s_loc = jnp.where(col_mask[:, None, :], s_loc + bias, -jnp.inf)
        s_ctx = jnp.einsum('bhqd,bhcd->bhqc', q_i, k_ctx).astype(jnp.float32) * scale
        logits = jnp.concatenate([s_loc.reshape(B, N_HEADS, GRID_W, kr * GRID_W), s_ctx], axis=-1)
        p = jax.nn.softmax(logits, axis=-1)
        p_loc = p[..., :kr * GRID_W].reshape(B, N_HEADS, GRID_W, kr, GRID_W).astype(v.dtype)
        p_ctx = p[..., kr * GRID_W:].astype(v.dtype)
        return (jnp.einsum('bhqak,bhakd->bhqd', p_loc, v_blk)
                + jnp.einsum('bhqc,bhcd->bhqd', p_ctx, v_ctx))

    o = lax.map(row_block, jnp.arange(rows))
    o = o.transpose(1, 0, 3, 2, 4).reshape(B, N, D_MODEL)
    return o @ w_out + b_out


def mlp(h, w1, b1, w2, b2):
    return jnp.square(jax.nn.relu(h @ w1 + b1)) @ w2 + b2


def setup_inputs(seed: int = 0) -> dict:
    key = jax.random.key(seed)
    ks = iter(jax.random.split(key, 40))
    f32 = jnp.float32

    def nrm(shape, s):
        return jax.random.normal(next(ks), shape, f32) * s

    inp = {}
    inp['x_prompt'] = nrm((BATCH, SEQ, D_MODEL), 1.0)
    inp['x_sample'] = nrm((DEC_BATCH, DEC_SEQ, D_MODEL), 1.0)
    inp['c'] = nrm((DEC_BATCH, D_MODEL), 1.0)
    inp['state_lru'] = nrm((DEC_BATCH, N_EVEN, 2, D_RNN), 0.5)
    inp['cache_k'] = nrm((DEC_BATCH, N_ODD, N_HEADS, PAST_LEN, HEAD_DIM), 1.0)
    inp['cache_v'] = nrm((DEC_BATCH, N_ODD, N_HEADS, PAST_LEN, HEAD_DIM), 1.0)
    inp['c_ctx'] = nrm((D_MODEL,), 1.0)
    inp['ada_w'] = nrm((DEPTH, D_MODEL, 6 * D_MODEL), 0.5 * D_MODEL ** -0.5)
    inp['ada_b'] = nrm((DEPTH, 6 * D_MODEL), 0.02)
    inp['ln1_g'] = 1.0 + nrm((DEPTH, D_MODEL), 0.02)
    inp['ln1_b'] = nrm((DEPTH, D_MODEL), 0.02)
    inp['ln2_g'] = 1.0 + nrm((DEPTH, D_MODEL), 0.02)
    inp['ln2_b'] = nrm((DEPTH, D_MODEL), 0.02)
    inp['w1'] = nrm((DEPTH, D_MODEL, D_FF), D_MODEL ** -0.5)
    inp['b1'] = nrm((DEPTH, D_FF), 0.02)
    inp['w2'] = nrm((DEPTH, D_FF, D_MODEL), BETA * D_FF ** -0.5)
    inp['b2'] = nrm((DEPTH, D_MODEL), 0.02)
    inp['e_w_in'] = nrm((N_EVEN, D_MODEL, D_IN_EVEN), D_MODEL ** -0.5)
    inp['e_b_in'] = nrm((N_EVEN, D_IN_EVEN), 0.02)
    inp['e_conv_w'] = nrm((N_EVEN, CONV_W, D_RNN), CONV_W ** -0.5)
    inp['e_conv_b'] = nrm((N_EVEN, D_RNN), 0.02)
    inp['e_w_r'] = nrm((N_EVEN, 2, RNN_HEADS, RNN_BLOCK, RNN_BLOCK), RNN_BLOCK ** -0.5)
    inp['e_b_r'] = nrm((N_EVEN, 2, D_RNN), 0.02)
    inp['e_w_i'] = nrm((N_EVEN, 2, RNN_HEADS, RNN_BLOCK, RNN_BLOCK), RNN_BLOCK ** -0.5)
    inp['e_b_i'] = nrm((N_EVEN, 2, D_RNN), 0.02)
    a0 = jax.random.uniform(next(ks), (N_EVEN, 2, D_RNN), f32, 0.9, 0.999)
    a_root = a0 ** (1.0 / C_SCALE)
    inp['e_lam'] = jnp.log(a_root) - jnp.log1p(-a_root)
    inp['e_w_out'] = nrm((N_EVEN, D_OUT_EVEN, D_MODEL), BETA * D_OUT_EVEN ** -0.5)
    inp['e_b_out'] = nrm((N_EVEN, D_MODEL), 0.02)
    inp['o_w_qkv'] = nrm((N_ODD, D_MODEL, 3 * D_MODEL), D_MODEL ** -0.5)
    inp['o_b_qkv'] = nrm((N_ODD, 3 * D_MODEL), 0.02)
    inp['o_rpb'] = nrm((N_ODD, N_HEADS, 2 * NA_ROWS - 1, 2 * NA_COLS - 1), 0.1)
    inp['o_w_out'] = nrm((N_ODD, D_MODEL, D_MODEL), BETA * D_MODEL ** -0.5)
    inp['o_b_out'] = nrm((N_ODD, D_MODEL), 0.02)
    return inp


def reference(x_prompt, x_sample, c, state_lru, cache_k, cache_v, c_ctx,
              ada_w, ada_b, ln1_g, ln1_b, ln2_g, ln2_b, w1, b1, w2, b2,
              e_w_in, e_b_in, e_conv_w, e_conv_b, e_w_r, e_b_r, e_w_i, e_b_i, e_lam,
              e_w_out, e_b_out, o_w_qkv, o_b_qkv, o_rpb, o_w_out, o_b_out):
    xp = x_prompt
    xs = x_sample
    new_lru = []
    new_k = []
    new_v = []
    for layer in range(DEPTH):
        mp = ada_params(c_ctx, ada_w[layer], ada_b[layer])
        ms = ada_params(c, ada_w[layer], ada_b[layer])
        hp = modulate(xp, mp[0], mp[1])
        hs = modulate(xs, ms[0], ms[1])
        if layer % 2 == 0:
            j = layer // 2
            prm = (e_w_in[j], e_b_in[j], e_conv_w[j], e_conv_b[j], e_w_r[j], e_b_r[j],
                   e_w_i[j], e_b_i[j], e_lam[j], e_w_out[j], e_b_out[j])
            h0_ctx = jnp.zeros((xp.shape[0], 2, D_RNN), xp.dtype)
            yp, st_ctx = mix_even(hp, h0_ctx, *prm)
            ys, _ = mix_even(hs, state_lru[:, j], *prm)
            new_lru.append(st_ctx)
        else:
            j = layer // 2
            yp, k_ctx, v_ctx = attn_context(hp, o_w_qkv[j], o_b_qkv[j], o_w_out[j], o_b_out[j])
            ys = attn_latent(hs, cache_k[:, j], cache_v[:, j], o_w_qkv[j], o_b_qkv[j],
                             o_rpb[j], o_w_out[j], o_b_out[j])
            new_k.append(k_ctx)
            new_v.append(v_ctx)
        xp = layer_norm(ALPHA * xp + mp[2] * yp, ln1_g[layer], ln1_b[layer])
        xs = layer_norm(ALPHA * xs + ms[2] * ys, ln1_g[layer], ln1_b[layer])
        hp = modulate(xp, mp[3], mp[4])
        hs = modulate(xs, ms[3], ms[4])
        xp = layer_norm(ALPHA * xp + mp[5] * mlp(hp, w1[layer], b1[layer], w2[layer], b2[layer]),
                        ln2_g[layer], ln2_b[layer])
        xs = layer_norm(ALPHA * xs + ms[5] * mlp(hs, w1[layer], b1[layer], w2[layer], b2[layer]),
                        ln2_g[layer], ln2_b[layer])
    new_state_lru = jnp.stack(new_lru, axis=1)
    new_cache_k = jnp.stack(new_k, axis=1)
    new_cache_v = jnp.stack(new_v, axis=1)
    return (xp, xs, new_state_lru, new_cache_k, new_cache_v)
```

```python
import functools

import numpy as np
import jax
import jax.numpy as jnp
from jax import lax
from jax.experimental import pallas as pl
from jax.experimental.pallas import tpu as pltpu

D_MODEL = 1024
DEPTH = 2
GRID_W = 64
N_HEADS = 16
HEAD_DIM = D_MODEL // N_HEADS
D_RNN = D_MODEL // 2
RNN_HEADS = 8
RNN_BLOCK = D_RNN // RNN_HEADS
CONV_W = 4
D_FOURIER = D_MODEL // 2
FOURIER_GROUPS = 4
FOURIER_CH = D_FOURIER // FOURIER_GROUPS
D_IN_EVEN = 2 * D_RNN + D_FOURIER
D_FF = 4 * D_MODEL
NA_ROWS = 8
NA_COLS = 16
C_SCALE = 8.0
ALPHA = (2 * DEPTH) ** 0.25
LN_EPS = 1e-5

F32 = jnp.float32
BF16 = jnp.bfloat16

SUBLANES = 8
VMEM_LIMIT = 56 * 1024 * 1024
TOKEN_BLOCK = 512
SCAN_CHUNK = 256
FF_CHUNK = 1024
FOURIER_SPLIT = 64
HEAD_PAIR = 2 * HEAD_DIM
N_PAIRS = N_HEADS // 2
BIAS_ROWS = 2 * NA_ROWS - 2
MOD_ROWS = 8


def _params(*sem):
    return pltpu.CompilerParams(dimension_semantics=sem, vmem_limit_bytes=VMEM_LIMIT)


def _const_spec(shape):
    zeros = (0,) * len(shape)
    return pl.BlockSpec(shape, lambda *_: zeros, pipeline_mode=pl.Buffered(1))


def _ln(x):
    mu = jnp.mean(x, axis=-1, keepdims=True)
    xc = x - mu
    var = jnp.mean(xc * xc, axis=-1, keepdims=True)
    return xc * lax.rsqrt(var + LN_EPS)


def _dot(a, b):
    return jnp.dot(a, b, preferred_element_type=F32)


def _dot_t(a, b):
    return lax.dot_general(a, b, (((1,), (1,)), ((), ())), preferred_element_type=F32)


def _ada_kernel(cond_ref, w_ref, b_ref, out_ref):
    cnd = cond_ref[...]
    s = cnd * jax.nn.sigmoid(cnd)
    out_ref[0] = jnp.dot(s, w_ref[0], preferred_element_type=F32,
                         precision=lax.Precision.HIGHEST) + b_ref[0]


def _ada_params(cond, ada_w, ada_b):
    out = pl.pallas_call(
        _ada_kernel,
        grid=(DEPTH, 6),
        in_specs=[
            pl.BlockSpec((MOD_ROWS, D_MODEL), lambda l, k: (0, 0)),
            pl.BlockSpec((1, D_MODEL, D_MODEL), lambda l, k: (l, 0, k)),
            pl.BlockSpec((1, 1, D_MODEL), lambda l, k: (l, 0, k)),
        ],
        out_specs=pl.BlockSpec((1, MOD_ROWS, D_MODEL), lambda l, k: (l, 0, k)),
        out_shape=jax.ShapeDtypeStruct((DEPTH, MOD_ROWS, 6 * D_MODEL), F32),
        compiler_params=_params("arbitrary", "arbitrary"),
        name="ada_params",
    )(cond, ada_w, ada_b.reshape(DEPTH, 1, 6 * D_MODEL))
    return out.reshape(DEPTH, MOD_ROWS, 6, D_MODEL)


def _mod_spec(blocks_per_batch):
    if blocks_per_batch:
        return pl.BlockSpec((1, 6, D_MODEL), lambda i: (i // blocks_per_batch, 0, 0))
    return pl.BlockSpec((1, 6, D_MODEL), lambda i: (0, 0, 0))


def _modulate(x, mod_ref, shift_row, scale_row):
    return (_ln(x) * (1.0 + mod_ref[0, scale_row:scale_row + 1, :])
            + mod_ref[0, shift_row:shift_row + 1, :])


def _even_in_kernel(x_ref, mod_ref, w_ref, b_ref, cc_ref, sc_ref, urg_ref, zc_ref, zs_ref):
    h = _modulate(x_ref[...], mod_ref, 0, 1)
    u = _dot(h.astype(BF16), w_ref[...]) + b_ref[...]
    urg_ref[...] = u[:, :2 * D_RNN]
    uf = u[:, 2 * D_RNN:].astype(BF16)
    zc_ref[...] = _dot(uf, cc_ref[...].astype(BF16)).astype(BF16)
    zs_ref[...] = _dot(uf, sc_ref[...].astype(BF16)).astype(BF16)


def _even_in(x, mod, blocks_per_batch, w_in, b_in, cc, sc):
    t = x.shape[0]
    tok = lambda n: pl.BlockSpec((TOKEN_BLOCK, n), lambda i: (i, 0))
    return pl.pallas_call(
        _even_in_kernel,
        grid=(t // TOKEN_BLOCK,),
        in_specs=[tok(D_MODEL), _mod_spec(blocks_per_batch),
                  _const_spec((D_MODEL, D_IN_EVEN)), _const_spec((1, D_IN_EVEN)),
                  _const_spec((D_FOURIER, D_FOURIER)), _const_spec((D_FOURIER, D_FOURIER))],
        out_specs=[tok(2 * D_RNN), tok(D_FOURIER), tok(D_FOURIER)],
        out_shape=[jax.ShapeDtypeStruct((t, 2 * D_RNN), F32),
                   jax.ShapeDtypeStruct((t, D_FOURIER), BF16),
                   jax.ShapeDtypeStruct((t, D_FOURIER), BF16)],
        compiler_params=_params("arbitrary"),
        name="even_in",
    )(x, mod, w_in, b_in, cc, sc)


def _softplus(x):
    return jnp.maximum(x, 0.0) + jnp.log1p(jnp.exp(-jnp.abs(x)))


def _conv(cur, prev8, next8, w_ref, b_ref):
    n = cur.shape[0]
    ext = jnp.concatenate([prev8, cur, next8], axis=0)
    out = b_ref[...]
    for k in range(CONV_W):
        shift = CONV_W // 2 - k
        if shift == 0:
            tap = cur
        else:
            tap = pltpu.roll(ext, shift % ext.shape[0], 0)[SUBLANES:SUBLANES + n]
        out = out + tap * w_ref[k:k + 1, :]
    return out


def _gates(xc, d, wg_ref, bg_ref, lam_ref, a_ref, b_ref):
    g = _dot(xc.astype(BF16), wg_ref[d]) + bg_ref[d]
    r = jax.nn.sigmoid(g[:, :D_RNN])
    i = jax.nn.sigmoid(g[:, D_RNN:])
    log_a = (-C_SCALE * r) * _softplus(-lam_ref[d])
    a = jnp.exp(log_a)
    a_ref[...] = a
    b_ref[...] = jnp.sqrt(-jnp.tanh(log_a) * (a * a + 1.0)) * (i * xc)


def _scan(a_ref, b_ref, out_ref, carry, reverse):
    n_tiles = a_ref.shape[0] // SUBLANES
    row = lax.broadcasted_iota(jnp.int32, (SUBLANES, D_RNN), 0)

    def body(t, h):
        tile = (n_tiles - 1 - t) if reverse else t
        r0 = pl.multiple_of(tile * SUBLANES, SUBLANES)
        a = a_ref[pl.ds(r0, SUBLANES), :]
        b = b_ref[pl.ds(r0, SUBLANES), :]
        for k in (1, 2, 4):
            if reverse:
                shift, valid = SUBLANES - k, row < SUBLANES - k
            else:
                shift, valid = k, row >= k
            a_sh = pltpu.roll(a, shift, 0)
            b_sh = pltpu.roll(b, shift, 0)
            b = jnp.where(valid, a * b_sh + b, b)
            a = jnp.where(valid, a * a_sh, a)
        hh = a * h + b
        out_ref[0, pl.ds(r0, SUBLANES), :] = hh
        return hh[0:1] if reverse else hh[SUBLANES - 1:SUBLANES]

    return lax.fori_loop(0, n_tiles, body, carry)


def _rglru_kernel(multi, *refs):
    if multi:
        (xf_ref, xfp_ref, xfn_ref, xb_ref, xbp_ref, xbn_ref, cw_ref, cb_ref, wg_ref, bg_ref,
         lam_ref, h0_ref, hf_ref, hb_ref, st_ref, af_ref, bf_ref, ab_ref, bb_ref, cf_ref,
         cbk_ref) = refs
    else:
        (xf_ref, cw_ref, cb_ref, wg_ref, bg_ref, lam_ref, h0_ref, hf_ref, hb_ref, st_ref,
         af_ref, bf_ref, ab_ref, bb_ref, cf_ref, cbk_ref) = refs
    j = pl.program_id(1)
    n = pl.num_programs(1)

    @pl.when(j == 0)
    def _():
        cf_ref[...] = h0_ref[0, 0:1, :]
        cbk_ref[...] = h0_ref[0, 1:2, :]

    zeros8 = jnp.zeros((SUBLANES, D_RNN), F32)
    if multi:
        xc_f = _conv(xf_ref[0], jnp.where(j > 0, xfp_ref[0], zeros8),
                     jnp.where(j < n - 1, xfn_ref[0], zeros8), cw_ref, cb_ref)
        xc_b = _conv(xb_ref[0], jnp.where(j < n - 1, xbp_ref[0], zeros8),
                     jnp.where(j > 0, xbn_ref[0], zeros8), cw_ref, cb_ref)
    else:
        xc_f = _conv(xf_ref[0], zeros8, zeros8, cw_ref, cb_ref)
        xc_b = xc_f
    _gates(xc_f, 0, wg_ref, bg_ref, lam_ref, af_ref, bf_ref)
    _gates(xc_b, 1, wg_ref, bg_ref, lam_ref, ab_ref, bb_ref)
    cf = _scan(af_ref, bf_ref, hf_ref, cf_ref[...], reverse=False)
    cb = _scan(ab_ref, bb_ref, hb_ref, cbk_ref[...], reverse=True)
    cf_ref[...] = cf
    cbk_ref[...] = cb
    st_ref[0, 0:1, :] = cf
    st_ref[0, 1:2, :] = cb


def _rglru(urg, h0, conv_w, conv_b, wg, bg, lam):
    b, s, _ = urg.shape
    c = SCAN_CHUNK
    n = s // c
    multi = n > 1
    tiles = c // SUBLANES
    last_tile = s // SUBLANES - 1
    chunk = lambda f: pl.BlockSpec((1, c, D_RNN), lambda bi, j: (bi, f(j), 0))
    halo = lambda f: pl.BlockSpec((1, SUBLANES, D_RNN), lambda bi, j: (bi, f(j), 0))
    fwd = lambda j: j
    bwd = lambda j: n - 1 - j
    if multi:
        x_specs = [chunk(fwd),
                   halo(lambda j: jnp.maximum(j * tiles - 1, 0)),
                   halo(lambda j: jnp.minimum((j + 1) * tiles, last_tile)),
                   chunk(bwd),
                   halo(lambda j: jnp.maximum((n - 1 - j) * tiles - 1, 0)),
                   halo(lambda j: jnp.minimum((n - j) * tiles, last_tile))]
        x_args = [urg] * 6
    else:
        x_specs = [chunk(fwd)]
        x_args = [urg]
    small = lambda shape: pl.BlockSpec(shape, lambda bi, j: (0,) * len(shape))
    return pl.pallas_call(
        functools.partial(_rglru_kernel, multi),
        grid=(b, n),
        in_specs=x_specs + [small((CONV_W, D_RNN)), small((1, D_RNN)),
                            small((2, D_RNN, 2 * D_RNN)), small((2, 1, 2 * D_RNN)),
                            small((2, 1, D_RNN)),
                            pl.BlockSpec((1, 2, D_RNN), lambda bi, j: (bi, 0, 0))],
        out_specs=[chunk(fwd), chunk(bwd),
                   pl.BlockSpec((1, 2, D_RNN), lambda bi, j: (bi, 0, 0))],
        out_shape=[jax.ShapeDtypeStruct((b, s, D_RNN), F32),
                   jax.ShapeDtypeStruct((b, s, D_RNN), F32),
                   jax.ShapeDtypeStruct((b, 2, D_RNN), F32)],
        scratch_shapes=[pltpu.VMEM((c, D_RNN), F32)] * 4 + [pltpu.VMEM((1, D_RNN), F32)] * 2,
        compiler_params=_params("arbitrary", "arbitrary"),
        name="rglru",
    )(*x_args, conv_w, conv_b, wg, bg, lam, h0)


def _fourier_kernel(split, *refs):
    if split:
        ca_ref, nsa_ref, cb_ref, nsb_ref, zc_ref, zs_ref, out_ref, cs_ref, nss_ref = refs
    else:
        cb_ref, nsb_ref, zc_ref, zs_ref, out_ref, cs_ref, nss_ref = refs
    i = pl.program_id(0)

    @pl.when(pl.program_id(1) == 0)
    def _():
        if split:
            q = cb_ref.shape[0]
            cb = cb_ref[...]
            nsb = nsb_ref[...]
            per_block = cs_ref.shape[0] // q
            for hh in range(per_block):
                ca = ca_ref[pl.ds(i * per_block + hh, 1), :]
                nsa = nsa_ref[pl.ds(i * per_block + hh, 1), :]
                cs_ref[hh * q:(hh + 1) * q, :] = (ca * cb - nsa * nsb).astype(BF16)
                nss_ref[hh * q:(hh + 1) * q, :] = (nsa * cb + ca * nsb).astype(BF16)
        else:
            cs_ref[...] = cb_ref[...].astype(BF16)
            nss_ref[...] = nsb_ref[...].astype(BF16)

    out_ref[0] = (_dot(cs_ref[...], zc_ref[0]) + _dot(nss_ref[...], zs_ref[0])).astype(BF16)


def _fourier(zc, zs):
    b, s, _ = zc.shape
    tf = min(s, TOKEN_BLOCK)
    split = s > tf
    tables = _position_tables(s, FOURIER_SPLIT if split else s)
    seq = pl.BlockSpec((1, s, D_FOURIER), lambda t, bi: (bi, 0, 0))
    return pl.pallas_call(
        functools.partial(_fourier_kernel, split),
        grid=(s // tf, b),
        in_specs=[_const_spec(tb.shape) for tb in tables] + [seq, seq],
        out_specs=pl.BlockSpec((1, tf, D_FOURIER), lambda t, bi: (bi, t, 0)),
        out_shape=jax.ShapeDtypeStruct((b, s, D_FOURIER), BF16),
        scratch_shapes=[pltpu.VMEM((tf, s), BF16)] * 2,
        compiler_params=_params("arbitrary", "arbitrary"),
        name="fourier",
    )(*tables, zc, zs)


def _post_kernel(even, *refs):
    if even:
        (x_ref, mod_ref, hf_ref, hb_ref, gate_ref, yb_ref, woa_ref, wob_ref, bo_ref,
         l1g_ref, l1b_ref, w1_ref, b1_ref, w2_ref, b2_ref, l2g_ref, l2b_ref, out_ref) = refs
        ya = (hf_ref[...] + hb_ref[...]) * jax.nn.gelu(gate_ref[...])
        y = _dot(ya.astype(BF16), woa_ref[...]) + _dot(yb_ref[...], wob_ref[...]) + bo_ref[...]
    else:
        (x_ref, mod_ref, o_ref, wo_ref, bo_ref,
         l1g_ref, l1b_ref, w1_ref, b1_ref, w2_ref, b2_ref, l2g_ref, l2b_ref, out_ref) = refs
        y = _dot(o_ref[...], wo_ref[...]) + bo_ref[...]
    x1 = _ln(ALPHA * x_ref[...] + mod_ref[0, 2:3, :] * y) * l1g_ref[...] + l1b_ref[...]
    hm = _modulate(x1, mod_ref, 3, 4).astype(BF16)
    acc = jnp.zeros((x1.shape[0], D_MODEL), F32)
    for c0 in range(0, D_FF, FF_CHUNK):
        t = _dot(hm, w1_ref[:, c0:c0 + FF_CHUNK]) + b1_ref[:, c0:c0 + FF_CHUNK]
        t = jnp.square(jnp.maximum(t, 0.0)).astype(BF16)
        acc = acc + _dot(t, w2_ref[c0:c0 + FF_CHUNK, :])
    mlp = acc + b2_ref[...]
    out_ref[...] = _ln(ALPHA * x1 + mod_ref[0, 5:6, :] * mlp) * l2g_ref[...] + l2b_ref[...]


def _post(even, x, mod, blocks_per_batch, acts, mix_weights, tail_weights):
    t = x.shape[0]
    tok = lambda n, col=0: pl.BlockSpec((TOKEN_BLOCK, n), lambda i: (i, col))
    act_specs = [tok(width, col) for _, width, col in acts]
    consts = list(mix_weights) + list(tail_weights)
    return pl.pallas_call(
        functools.partial(_post_kernel, even),
        grid=(t // TOKEN_BLOCK,),
        in_specs=[tok(D_MODEL), _mod_spec(blocks_per_batch)] + act_specs
                 + [_const_spec(w.shape) for w in consts],
        out_specs=tok(D_MODEL),
        out_shape=jax.ShapeDtypeStruct((t, D_MODEL), F32),
        compiler_params=_params("arbitrary"),
        name="post_even" if even else "post_odd",
    )(x, mod, *[a for a, _, _ in acts], *consts)


def _qkv_kernel(x_ref, mod_ref, w_ref, b_ref, q_ref, k_ref, v_ref):
    h = _modulate(x_ref[...], mod_ref, 0, 1)
    u = _dot(h.astype(BF16), w_ref[...]) + b_ref[...]
    q_ref[...] = u[:, :D_MODEL].astype(q_ref.dtype)
    k_ref[...] = u[:, D_MODEL:2 * D_MODEL].astype(k_ref.dtype)
    v_ref[...] = u[:, 2 * D_MODEL:].astype(v_ref.dtype)


def _qkv(x, mod, blocks_per_batch, w_qkv, b_qkv, kv_dtype):
    t = x.shape[0]
    tok = pl.BlockSpec((TOKEN_BLOCK, D_MODEL), lambda i: (i, 0))
    return pl.pallas_call(
        _qkv_kernel,
        grid=(t // TOKEN_BLOCK,),
        in_specs=[tok, _mod_spec(blocks_per_batch),
                  _const_spec((D_MODEL, 3 * D_MODEL)), _const_spec((1, 3 * D_MODEL))],
        out_specs=[tok, tok, tok],
        out_shape=[jax.ShapeDtypeStruct((t, D_MODEL), BF16),
                   jax.ShapeDtypeStruct((t, D_MODEL), kv_dtype),
                   jax.ShapeDtypeStruct((t, D_MODEL), kv_dtype)],
        compiler_params=_params("arbitrary"),
        name="qkv",
    )(x, mod, w_qkv, b_qkv)


def _head_mask(rows):
    return lax.broadcasted_iota(jnp.int32, (rows, HEAD_PAIR), 1) >= HEAD_DIM


def _ctx_attn_kernel(q_ref, k_ref, v_ref, o_ref):
    q = q_ref[0] * (HEAD_DIM ** -0.5)
    k = k_ref[0].astype(BF16)
    v = v_ref[0].astype(BF16)
    second = _head_mask(q.shape[0])
    out = None
    for hh in range(2):
        own = second if hh else jnp.logical_not(second)
        s = _dot_t(jnp.where(own, q, 0.0).astype(BF16), k)
        m = jnp.max(s, axis=-1, keepdims=True)
        p = jnp.exp(s - m)
        l = jnp.sum(p, axis=-1, keepdims=True)
        o = _dot(p.astype(BF16), v) / l
        out = o if out is None else jnp.where(own, o, out)
    o_ref[0] = out.astype(o_ref.dtype)


def _ctx_attn(q, k, v):
    b, s, _ = q.shape
    spec = pl.BlockSpec((1, s, HEAD_PAIR), lambda bi, hp: (bi, 0, hp))
    return pl.pallas_call(
        _ctx_attn_kernel,
        grid=(b, N_PAIRS),
        in_specs=[spec, spec, spec],
        out_specs=spec,
        out_shape=jax.ShapeDtypeStruct((b, s, D_MODEL), BF16),
        compiler_params=_params("arbitrary", "arbitrary"),
        name="ctx_attn",
    )(q, k, v)


def _rpb_kernel(rpb_ref, out_ref):
    h = pl.program_id(0)
    shape = (GRID_W, HEAD_PAIR)
    qcol = lax.broadcasted_iota(jnp.int32, shape, 0)
    lane = lax.broadcasted_iota(jnp.int32, shape, 1)
    second = lane >= GRID_W
    kcol = jnp.where(second, lane - GRID_W, lane)
    off = jnp.clip(kcol - qcol + NA_COLS - 1, 0, 2 * NA_COLS - 2)
    start = jnp.clip(qcol - NA_COLS // 2, 0, GRID_W - NA_COLS)
    visible = jnp.logical_and(kcol >= start, kcol < start + NA_COLS)
    n_off = 2 * NA_COLS - 1
    base = h * ((2 * NA_ROWS - 1) * n_off)
    for ro in range(BIAS_ROWS):
        t = jnp.zeros(shape, F32)
        for dd in range(n_off):
            val = jnp.where(second, rpb_ref[base + (ro + 1) * n_off + dd],
                            rpb_ref[base + ro * n_off + dd])
            t = jnp.where(off == dd, val, t)
        out_ref[0, ro] = jnp.where(visible, t, -jnp.inf)


def _rpb_tables(rpb):
    return pl.pallas_call(
        _rpb_kernel,
        grid=(N_HEADS,),
        in_specs=[pl.BlockSpec(memory_space=pltpu.SMEM)],
        out_specs=pl.BlockSpec((1, BIAS_ROWS, GRID_W, HEAD_PAIR), lambda h: (h, 0, 0, 0)),
        out_shape=jax.ShapeDtypeStruct((N_HEADS, BIAS_ROWS, GRID_W, HEAD_PAIR), F32),
        compiler_params=_params("arbitrary"),
        name="rpb_tables",
    )(rpb.reshape(-1))


def _lat_attn_kernel(q_ref, k_ref, v_ref, kc_ref, vc_ref, bias_ref, o_ref):
    rows = q_ref.shape[1] // GRID_W
    window = NA_ROWS * GRID_W
    second = _head_mask(GRID_W)
    kc = kc_ref[0, 0]
    vc = vc_ref[0, 0]

    def body(i, carry):
        rs = jnp.clip(i - NA_ROWS // 2, 0, rows - NA_ROWS)
        q0 = pl.multiple_of(i * GRID_W, GRID_W)
        k0 = pl.multiple_of(rs * GRID_W, GRID_W)
        ro = rs - i + NA_ROWS - 1
        q = q_ref[0, pl.ds(q0, GRID_W), :] * (HEAD_DIM ** -0.5)
        kb = k_ref[0, pl.ds(k0, window), :]
        vb = v_ref[0, pl.ds(k0, window), :]
        out = None
        for hh in range(2):
            own = second if hh else jnp.logical_not(second)
            qm = jnp.where(own, q, 0.0).astype(BF16)
            bias = jnp.concatenate([bias_ref[hh, ro + 2 * a] for a in range(NA_ROWS // 2)], axis=1)
            s_loc = _dot_t(qm, kb) + bias
            s_ctx = _dot_t(qm, kc)
            m = jnp.maximum(jnp.max(s_loc, axis=-1, keepdims=True),
                            jnp.max(s_ctx, axis=-1, keepdims=True))
            p_loc = jnp.exp(s_loc - m)
            p_ctx = jnp.exp(s_ctx - m)
            l = jnp.sum(p_loc, axis=-1, keepdims=True) + jnp.sum(p_ctx, axis=-1, keepdims=True)
            o = (_dot(p_loc.astype(BF16), vb) + _dot(p_ctx.astype(BF16), vc)) / l
            out = o if out is None else jnp.where(own, o, out)
        o_ref[0, pl.ds(q0, GRID_W), :] = out.astype(o_ref.dtype)
        return carry

    lax.fori_loop(0, rows, body, 0)


def _lat_attn(q, k, v, kc, vc, bias):
    b, n, _ = q.shape
    lc = kc.shape[2]
    seq = pl.BlockSpec((1, n, HEAD_PAIR), lambda bi, hp: (bi, 0, hp))
    ctx = pl.BlockSpec((1, 1, lc, HEAD_PAIR), lambda bi, hp: (bi, hp, 0, 0))
    return pl.pallas_call(
        _lat_attn_kernel,
        grid=(b, N_PAIRS),
        in_specs=[seq, seq, seq, ctx, ctx,
                  pl.BlockSpec((2, BIAS_ROWS, GRID_W, HEAD_PAIR), lambda bi, hp: (hp, 0, 0, 0))],
        out_specs=seq,
        out_shape=jax.ShapeDtypeStruct((b, n, D_MODEL), BF16),
        compiler_params=_params("arbitrary", "arbitrary"),
        name="lat_attn",
    )(q, k, v, kc, vc, bias)


def _angles(rows, cols, period):
    theta = 2.0 * np.pi * ((rows[:, None] * cols[None, :]) % period) / period
    return np.cos(theta), np.sin(theta)


def _channel_dft():
    idx = np.arange(FOURIER_CH)
    c, s = _angles(idx, idx, FOURIER_CH)
    eye = np.eye(FOURIER_GROUPS)
    scale = FOURIER_CH ** -0.5
    return jnp.asarray(np.kron(eye, c) * scale, F32), jnp.asarray(np.kron(eye, s) * scale, F32)


def _position_tables(n, q):
    pos = np.arange(n)
    cb, sb = _angles(np.arange(q), pos, n)
    scale = n ** -0.5
    b_tables = [jnp.asarray(cb * scale, F32), jnp.asarray(-sb * scale, F32)]
    if q == n:
        return b_tables
    ca, sa = _angles(np.arange(n // q) * q, pos, n)
    return [jnp.asarray(ca, F32), jnp.asarray(-sa, F32)] + b_tables


def _block_diag(w):
    eye = jnp.eye(RNN_HEADS, dtype=w.dtype)
    return jnp.einsum('hij,hk->hikj', w, eye).reshape(D_RNN, D_RNN)


def _pair_heads(cache):
    b, h, l, d = cache.shape
    return cache.reshape(b, h // 2, 2, l, d).transpose(0, 1, 3, 2, 4).reshape(
        b, h // 2, l, 2 * d).astype(BF16)


def _split_heads(t, b, s):
    return t.reshape(b, s, N_HEADS, HEAD_DIM).transpose(0, 2, 1, 3)


def kernel(x_prompt, x_sample, c, state_lru, cache_k, cache_v, c_ctx, ada_w, ada_b, ln1_g, ln1_b, ln2_g, ln2_b, w1, b1, w2, b2, e_w_in, e_b_in, e_conv_w, e_conv_b, e_w_r, e_b_r, e_w_i, e_b_i, e_lam, e_w_out, e_b_out, o_w_qkv, o_b_qkv, o_rpb, o_w_out, o_b_out):
    bp, sp, _ = x_prompt.shape
    bs, ss, _ = x_sample.shape
    row = lambda v: v.reshape(1, -1)

    cond = jnp.concatenate(
        [c_ctx[None], c, jnp.zeros((MOD_ROWS - 1 - bs, D_MODEL), F32)], axis=0)
    mod = _ada_params(cond, ada_w, ada_b)

    streams = [
        dict(x=x_prompt.reshape(bp * sp, D_MODEL), b=bp, s=sp, bpb=0, rows=slice(0, 1)),
        dict(x=x_sample.reshape(bs * ss, D_MODEL), b=bs, s=ss, bpb=ss // TOKEN_BLOCK,
             rows=slice(1, 1 + bs)),
    ]
    cc, sc = _channel_dft()
    new_lru, new_k, new_v = [], [], []

    for layer in range(DEPTH):
        j = layer // 2
        tail = [row(ln1_g[layer]), row(ln1_b[layer]), w1[layer].astype(BF16), row(b1[layer]),
                w2[layer].astype(BF16), row(b2[layer]), row(ln2_g[layer]), row(ln2_b[layer])]
        if layer % 2 == 0:
            w_in = e_w_in[j].astype(BF16)
            w_out = e_w_out[j].astype(BF16)
            wg = jnp.stack([jnp.concatenate([_block_diag(e_w_r[j, d]), _block_diag(e_w_i[j, d])],
                                            axis=1) for d in range(2)]).astype(BF16)
            bg = jnp.concatenate([e_b_r[j], e_b_i[j]], axis=-1)[:, None, :]
            lam = e_lam[j][:, None, :]
            mix = [w_out[:D_RNN], w_out[D_RNN:], row(e_b_out[j])]
            for st in streams:
                m = mod[layer, st['rows']]
                b, s = st['b'], st['s']
                urg, zc, zs = _even_in(st['x'], m, st['bpb'], w_in, row(e_b_in[j]), cc, sc)
                if st['bpb']:
                    h0 = state_lru[:, j]
                else:
                    h0 = jnp.zeros((b, 2, D_RNN), F32)
                hf, hb, fin = _rglru(urg.reshape(b, s, 2 * D_RNN), h0, e_conv_w[j],
                                     row(e_conv_b[j]), wg, bg, lam)
                yb = _fourier(zc.reshape(b, s, D_FOURIER), zs.reshape(b, s, D_FOURIER))
                acts = [(hf.reshape(b * s, D_RNN), D_RNN, 0), (hb.reshape(b * s, D_RNN), D_RNN, 0),
                        (urg, D_RNN, 1), (yb.reshape(b * s, D_FOURIER), D_FOURIER, 0)]
                st['x'] = _post(True, st['x'], m, st['bpb'], acts, mix, tail)
                if not st['bpb']:
                    new_lru.append(fin)
        else:
            w_qkv = o_w_qkv[j].astype(BF16)
            mix = [o_w_out[j].astype(BF16), row(o_b_out[j])]
            bias = _rpb_tables(o_rpb[j])
            for st in streams:
                m = mod[layer, st['rows']]
                b, s = st['b'], st['s']
                latent = bool(st['bpb'])
                q, k, v = _qkv(st['x'], m, st['bpb'], w_qkv, row(o_b_qkv[j]),
                               BF16 if latent else F32)
                shape = (b, s, D_MODEL)
                if latent:
                    o = _lat_attn(q.reshape(shape), k.reshape(shape), v.reshape(shape),
                                  _pair_heads(cache_k[:, j]), _pair_heads(cache_v[:, j]), bias)
                else:
                    o = _ctx_attn(q.reshape(shape), k.reshape(shape), v.reshape(shape))
                    new_k.append(_split_heads(k, b, s))
                    new_v.append(_split_heads(v, b, s))
                st['x'] = _post(False, st['x'], m, st['bpb'], [(o.reshape(b * s, D_MODEL), D_MODEL, 0)],
                                mix, tail)

    return (streams[0]['x'].reshape(bp, sp, D_MODEL), streams[1]['x'].reshape(bs, ss, D_MODEL),
            jnp.stack(new_lru, axis=1), jnp.stack(new_k, axis=1), jnp.stack(new_v, axis=1))
```

```python
import functools

import numpy as np
import jax
import jax.numpy as jnp
from jax import lax
from jax.experimental import pallas as pl
from jax.experimental.pallas import tpu as pltpu

D_MODEL = 1024
DEPTH = 2
GRID_W = 64
N_HEADS = 16
HEAD_DIM = D_MODEL // N_HEADS
D_RNN = D_MODEL // 2
RNN_HEADS = 8
RNN_BLOCK = D_RNN // RNN_HEADS
CONV_W = 4
D_FOURIER = D_MODEL // 2
FOURIER_GROUPS = 4
FOURIER_CH = D_FOURIER // FOURIER_GROUPS
D_IN_EVEN = 2 * D_RNN + D_FOURIER
D_FF = 4 * D_MODEL
NA_ROWS = 8
NA_COLS = 16
C_SCALE = 8.0
ALPHA = (2 * DEPTH) ** 0.25
LN_EPS = 1e-5

F32 = jnp.float32
BF16 = jnp.bfloat16

SUBLANES = 8
VMEM_LIMIT = 56 * 1024 * 1024
TOKEN_BLOCK = 512
SCAN_CHUNK = 256
FF_CHUNK = 1024
POST_SPLIT = 2
FOURIER_SPLIT = 64
HEAD_PAIR = 2 * HEAD_DIM
N_PAIRS = N_HEADS // 2
BIAS_ROWS = 2 * NA_ROWS - 2
MOD_ROWS = 8


def _params(*sem):
    return pltpu.CompilerParams(dimension_semantics=sem, vmem_limit_bytes=VMEM_LIMIT)


def _const_spec(shape):
    zeros = (0,) * len(shape)
    return pl.BlockSpec(shape, lambda *_: zeros, pipeline_mode=pl.Buffered(1))


def _ln(x):
    mu = jnp.mean(x, axis=-1, keepdims=True)
    xc = x - mu
    var = jnp.mean(xc * xc, axis=-1, keepdims=True)
    return xc * lax.rsqrt(var + LN_EPS)


def _dot(a, b):
    return jnp.dot(a, b, preferred_element_type=F32)


def _dot_t(a, b):
    return lax.dot_general(a, b, (((1,), (1,)), ((), ())), preferred_element_type=F32)


def _ada_kernel(cond_ref, w_ref, b_ref, out_ref):
    cnd = cond_ref[...]
    s = cnd * jax.nn.sigmoid(cnd)
    out_ref[0] = jnp.dot(s, w_ref[0], preferred_element_type=F32,
                         precision=lax.Precision.HIGHEST) + b_ref[0]


def _ada_params(cond, ada_w, ada_b):
    out = pl.pallas_call(
        _ada_kernel,
        grid=(DEPTH, 6),
        in_specs=[
            pl.BlockSpec((MOD_ROWS, D_MODEL), lambda l, k: (0, 0)),
            pl.BlockSpec((1, D_MODEL, D_MODEL), lambda l, k: (l, 0, k)),
            pl.BlockSpec((1, 1, D_MODEL), lambda l, k: (l, 0, k)),
        ],
        out_specs=pl.BlockSpec((1, MOD_ROWS, D_MODEL), lambda l, k: (l, 0, k)),
        out_shape=jax.ShapeDtypeStruct((DEPTH, MOD_ROWS, 6 * D_MODEL), F32),
        compiler_params=_params("arbitrary", "arbitrary"),
        name="ada_params",
    )(cond, ada_w, ada_b.reshape(DEPTH, 1, 6 * D_MODEL))
    return out.reshape(DEPTH, MOD_ROWS, 6, D_MODEL)


def _mod_spec(blocks_per_batch):
    if blocks_per_batch:
        return pl.BlockSpec((1, 6, D_MODEL), lambda i: (i // blocks_per_batch, 0, 0))
    return pl.BlockSpec((1, 6, D_MODEL), lambda i: (0, 0, 0))


def _modulate(x, mod_ref, shift_row, scale_row):
    return (_ln(x) * (1.0 + mod_ref[0, scale_row:scale_row + 1, :])
            + mod_ref[0, shift_row:shift_row + 1, :])


def _even_in_kernel(x_ref, mod_ref, w_ref, b_ref, cc_ref, sc_ref, urg_ref, zc_ref, zs_ref):
    h = _modulate(x_ref[...], mod_ref, 0, 1)
    u = _dot(h.astype(BF16), w_ref[...]) + b_ref[...]
    urg_ref[...] = u[:, :2 * D_RNN]
    uf = u[:, 2 * D_RNN:].astype(BF16)
    zc_ref[...] = _dot(uf, cc_ref[...].astype(BF16)).astype(BF16)
    zs_ref[...] = _dot(uf, sc_ref[...].astype(BF16)).astype(BF16)


def _even_in(x, mod, blocks_per_batch, w_in, b_in, cc, sc):
    t = x.shape[0]
    tok = lambda n: pl.BlockSpec((TOKEN_BLOCK, n), lambda i: (i, 0))
    return pl.pallas_call(
        _even_in_kernel,
        grid=(t // TOKEN_BLOCK,),
        in_specs=[tok(D_MODEL), _mod_spec(blocks_per_batch),
                  _const_spec((D_MODEL, D_IN_EVEN)), _const_spec((1, D_IN_EVEN)),
                  _const_spec((D_FOURIER, D_FOURIER)), _const_spec((D_FOURIER, D_FOURIER))],
        out_specs=[tok(2 * D_RNN), tok(D_FOURIER), tok(D_FOURIER)],
        out_shape=[jax.ShapeDtypeStruct((t, 2 * D_RNN), F32),
                   jax.ShapeDtypeStruct((t, D_FOURIER), BF16),
                   jax.ShapeDtypeStruct((t, D_FOURIER), BF16)],
        compiler_params=_params("arbitrary"),
        name="even_in",
    )(x, mod, w_in, b_in, cc, sc)


def _softplus(x):
    return jnp.maximum(x, 0.0) + jnp.log1p(jnp.exp(-jnp.abs(x)))


def _conv(cur, prev8, next8, w_ref, b_ref):
    n = cur.shape[0]
    ext = jnp.concatenate([prev8, cur, next8], axis=0)
    out = b_ref[...]
    for k in range(CONV_W):
        shift = CONV_W // 2 - k
        if shift == 0:
            tap = cur
        else:
            tap = pltpu.roll(ext, shift % ext.shape[0], 0)[SUBLANES:SUBLANES + n]
        out = out + tap * w_ref[k:k + 1, :]
    return out


def _gates(xc, d, wg_ref, bg_ref, lam_ref, a_ref, b_ref):
    g = _dot(xc.astype(BF16), wg_ref[d]) + bg_ref[d]
    r = jax.nn.sigmoid(g[:, :D_RNN])
    i = jax.nn.sigmoid(g[:, D_RNN:])
    log_a = (-C_SCALE * r) * _softplus(-lam_ref[d])
    a = jnp.exp(log_a)
    a_ref[...] = a
    b_ref[...] = jnp.sqrt(-jnp.tanh(log_a) * (a * a + 1.0)) * (i * xc)


def _scan(a_ref, b_ref, out_ref, carry, reverse):
    n_tiles = a_ref.shape[0] // SUBLANES
    row = lax.broadcasted_iota(jnp.int32, (SUBLANES, D_RNN), 0)

    def body(t, h):
        tile = (n_tiles - 1 - t) if reverse else t
        r0 = pl.multiple_of(tile * SUBLANES, SUBLANES)
        a = a_ref[pl.ds(r0, SUBLANES), :]
        b = b_ref[pl.ds(r0, SUBLANES), :]
        for k in (1, 2, 4):
            if reverse:
                shift, valid = SUBLANES - k, row < SUBLANES - k
            else:
                shift, valid = k, row >= k
            a_sh = pltpu.roll(a, shift, 0)
            b_sh = pltpu.roll(b, shift, 0)
            b = jnp.where(valid, a * b_sh + b, b)
            a = jnp.where(valid, a * a_sh, a)
        hh = a * h + b
        out_ref[0, pl.ds(r0, SUBLANES), :] = hh
        return hh[0:1] if reverse else hh[SUBLANES - 1:SUBLANES]

    return lax.fori_loop(0, n_tiles, body, carry)


def _rglru_kernel(multi, *refs):
    if multi:
        (xf_ref, xfp_ref, xfn_ref, xb_ref, xbp_ref, xbn_ref, cw_ref, cb_ref, wg_ref, bg_ref,
         lam_ref, h0_ref, hf_ref, hb_ref, st_ref, af_ref, bf_ref, ab_ref, bb_ref, cf_ref,
         cbk_ref) = refs
    else:
        (xf_ref, cw_ref, cb_ref, wg_ref, bg_ref, lam_ref, h0_ref, hf_ref, hb_ref, st_ref,
         af_ref, bf_ref, ab_ref, bb_ref, cf_ref, cbk_ref) = refs
    j = pl.program_id(1)
    n = pl.num_programs(1)

    @pl.when(j == 0)
    def _():
        cf_ref[...] = h0_ref[0, 0:1, :]
        cbk_ref[...] = h0_ref[0, 1:2, :]

    zeros8 = jnp.zeros((SUBLANES, D_RNN), F32)
    if multi:
        xc_f = _conv(xf_ref[0], jnp.where(j > 0, xfp_ref[0], zeros8),
                     jnp.where(j < n - 1, xfn_ref[0], zeros8), cw_ref, cb_ref)
        xc_b = _conv(xb_ref[0], jnp.where(j < n - 1, xbp_ref[0], zeros8),
                     jnp.where(j > 0, xbn_ref[0], zeros8), cw_ref, cb_ref)
    else:
        xc_f = _conv(xf_ref[0], zeros8, zeros8, cw_ref, cb_ref)
        xc_b = xc_f
    _gates(xc_f, 0, wg_ref, bg_ref, lam_ref, af_ref, bf_ref)
    _gates(xc_b, 1, wg_ref, bg_ref, lam_ref, ab_ref, bb_ref)
    cf = _scan(af_ref, bf_ref, hf_ref, cf_ref[...], reverse=False)
    cb = _scan(ab_ref, bb_ref, hb_ref, cbk_ref[...], reverse=True)
    cf_ref[...] = cf
    cbk_ref[...] = cb
    st_ref[0, 0:1, :] = cf
    st_ref[0, 1:2, :] = cb


def _rglru(urg, h0, conv_w, conv_b, wg, bg, lam):
    b, s, _ = urg.shape
    c = SCAN_CHUNK
    n = s // c
    multi = n > 1
    tiles = c // SUBLANES
    last_tile = s // SUBLANES - 1
    chunk = lambda f: pl.BlockSpec((1, c, D_RNN), lambda bi, j: (bi, f(j), 0))
    halo = lambda f: pl.BlockSpec((1, SUBLANES, D_RNN), lambda bi, j: (bi, f(j), 0))
    fwd = lambda j: j
    bwd = lambda j: n - 1 - j
    if multi:
        x_specs = [chunk(fwd),
                   halo(lambda j: jnp.maximum(j * tiles - 1, 0)),
                   halo(lambda j: jnp.minimum((j + 1) * tiles, last_tile)),
                   chunk(bwd),
                   halo(lambda j: jnp.maximum((n - 1 - j) * tiles - 1, 0)),
                   halo(lambda j: jnp.minimum((n - j) * tiles, last_tile))]
        x_args = [urg] * 6
    else:
        x_specs = [chunk(fwd)]
        x_args = [urg]
    small = lambda shape: pl.BlockSpec(shape, lambda bi, j: (0,) * len(shape))
    return pl.pallas_call(
        functools.partial(_rglru_kernel, multi),
        grid=(b, n),
        in_specs=x_specs + [small((CONV_W, D_RNN)), small((1, D_RNN)),
                            small((2, D_RNN, 2 * D_RNN)), small((2, 1, 2 * D_RNN)),
                            small((2, 1, D_RNN)),
                            pl.BlockSpec((1, 2, D_RNN), lambda bi, j: (bi, 0, 0))],
        out_specs=[chunk(fwd), chunk(bwd),
                   pl.BlockSpec((1, 2, D_RNN), lambda bi, j: (bi, 0, 0))],
        out_shape=[jax.ShapeDtypeStruct((b, s, D_RNN), F32),
                   jax.ShapeDtypeStruct((b, s, D_RNN), F32),
                   jax.ShapeDtypeStruct((b, 2, D_RNN), F32)],
        scratch_shapes=[pltpu.VMEM((c, D_RNN), F32)] * 4 + [pltpu.VMEM((1, D_RNN), F32)] * 2,
        compiler_params=_params("arbitrary", "arbitrary"),
        name="rglru",
    )(*x_args, conv_w, conv_b, wg, bg, lam, h0)


def _fourier_kernel(split, *refs):
    if split:
        ca_ref, nsa_ref, cb_ref, nsb_ref, zc_ref, zs_ref, out_ref, cs_ref, nss_ref = refs
    else:
        cb_ref, nsb_ref, zc_ref, zs_ref, out_ref, cs_ref, nss_ref = refs
    i = pl.program_id(0)

    @pl.when(pl.program_id(1) == 0)
    def _():
        if split:
            q = cb_ref.shape[0]
            cb = cb_ref[...]
            nsb = nsb_ref[...]
            per_block = cs_ref.shape[0] // q
            for hh in range(per_block):
                ca = ca_ref[pl.ds(i * per_block + hh, 1), :]
                nsa = nsa_ref[pl.ds(i * per_block + hh, 1), :]
                cs_ref[hh * q:(hh + 1) * q, :] = (ca * cb - nsa * nsb).astype(BF16)
                nss_ref[hh * q:(hh + 1) * q, :] = (nsa * cb + ca * nsb).astype(BF16)
        else:
            cs_ref[...] = cb_ref[...].astype(BF16)
            nss_ref[...] = nsb_ref[...].astype(BF16)

    out_ref[0] = (_dot(cs_ref[...], zc_ref[0]) + _dot(nss_ref[...], zs_ref[0])).astype(BF16)


def _fourier(zc, zs):
    b, s, _ = zc.shape
    tf = min(s, TOKEN_BLOCK)
    split = s > tf
    tables = _position_tables(s, FOURIER_SPLIT if split else s)
    seq = pl.BlockSpec((1, s, D_FOURIER), lambda t, bi: (bi, 0, 0))
    return pl.pallas_call(
        functools.partial(_fourier_kernel, split),
        grid=(s // tf, b),
        in_specs=[_const_spec(tb.shape) for tb in tables] + [seq, seq],
        out_specs=pl.BlockSpec((1, tf, D_FOURIER), lambda t, bi: (bi, t, 0)),
        out_shape=jax.ShapeDtypeStruct((b, s, D_FOURIER), BF16),
        scratch_shapes=[pltpu.VMEM((tf, s), BF16)] * 2,
        compiler_params=_params("arbitrary", "arbitrary"),
        name="fourier",
    )(*tables, zc, zs)


def _post_kernel(even, *refs):
    if even:
        (x_ref, mod_ref, hf_ref, hb_ref, gate_ref, yb_ref, woa_ref, wob_ref, bo_ref,
         l1g_ref, l1b_ref, w1_ref, b1_ref, w2_ref, b2_ref, l2g_ref, l2b_ref, out_ref) = refs
    else:
        (x_ref, mod_ref, o_ref, wo_ref, bo_ref,
         l1g_ref, l1b_ref, w1_ref, b1_ref, w2_ref, b2_ref, l2g_ref, l2b_ref, out_ref) = refs

    def mix(rows):
        if even:
            ya = (hf_ref[rows, :] + hb_ref[rows, :]) * jax.nn.gelu(gate_ref[rows, :])
            return (_dot(ya.astype(BF16), woa_ref[...]) + _dot(yb_ref[rows, :], wob_ref[...])
                    + bo_ref[...])
        return _dot(o_ref[rows, :], wo_ref[...]) + bo_ref[...]

    def mlp(hm):
        acc = jnp.zeros((hm.shape[0], D_MODEL), F32)
        for c0 in range(0, D_FF, FF_CHUNK):
            t = _dot(hm, w1_ref[:, c0:c0 + FF_CHUNK]) + b1_ref[:, c0:c0 + FF_CHUNK]
            t = jnp.square(jnp.maximum(t, 0.0)).astype(BF16)
            acc = acc + _dot(t, w2_ref[c0:c0 + FF_CHUNK, :])
        return acc + b2_ref[...]

    n = x_ref.shape[0] // POST_SPLIT
    groups = [pl.ds(g * n, n) for g in range(POST_SPLIT)]
    ys = [mix(rows) for rows in groups]
    x1s = [_ln(ALPHA * x_ref[rows, :] + mod_ref[0, 2:3, :] * y) * l1g_ref[...] + l1b_ref[...]
           for rows, y in zip(groups, ys)]
    hms = [_modulate(x1, mod_ref, 3, 4).astype(BF16) for x1 in x1s]
    for rows, x1, hm in zip(groups, x1s, hms):
        out_ref[rows, :] = (_ln(ALPHA * x1 + mod_ref[0, 5:6, :] * mlp(hm)) * l2g_ref[...]
                            + l2b_ref[...])


def _post(even, x, mod, blocks_per_batch, acts, mix_weights, tail_weights):
    t = x.shape[0]
    tok = lambda n, col=0: pl.BlockSpec((TOKEN_BLOCK, n), lambda i: (i, col))
    act_specs = [tok(width, col) for _, width, col in acts]
    consts = list(mix_weights) + list(tail_weights)
    return pl.pallas_call(
        functools.partial(_post_kernel, even),
        grid=(t // TOKEN_BLOCK,),
        in_specs=[tok(D_MODEL), _mod_spec(blocks_per_batch)] + act_specs
                 + [_const_spec(w.shape) for w in consts],
        out_specs=tok(D_MODEL),
        out_shape=jax.ShapeDtypeStruct((t, D_MODEL), F32),
        compiler_params=_params("arbitrary"),
        name="post_even" if even else "post_odd",
    )(x, mod, *[a for a, _, _ in acts], *consts)


def _qkv_kernel(x_ref, mod_ref, w_ref, b_ref, q_ref, k_ref, v_ref, *cache_refs):
    h = _modulate(x_ref[...], mod_ref, 0, 1)
    u = _dot(h.astype(BF16), w_ref[...]) + b_ref[...]
    q_ref[...] = u[:, :D_MODEL].astype(BF16)
    k_ref[...] = u[:, D_MODEL:2 * D_MODEL].astype(BF16)
    v_ref[...] = u[:, 2 * D_MODEL:].astype(BF16)
    for which, ref in enumerate(cache_refs):
        n_seq, _, s, _ = ref.shape
        for b in range(n_seq):
            for hd in range(N_HEADS):
                c0 = (1 + which) * D_MODEL + hd * HEAD_DIM
                ref[b, hd] = u[b * s:(b + 1) * s, c0:c0 + HEAD_DIM]


def _qkv(x, mod, blocks_per_batch, w_qkv, b_qkv, cache_seq):
    t = x.shape[0]
    tok = pl.BlockSpec((TOKEN_BLOCK, D_MODEL), lambda i: (i, 0))
    out_specs = [tok, tok, tok]
    out_shape = [jax.ShapeDtypeStruct((t, D_MODEL), BF16)] * 3
    if cache_seq:
        per_block = TOKEN_BLOCK // cache_seq
        out_specs += [pl.BlockSpec((per_block, N_HEADS, cache_seq, HEAD_DIM),
                                   lambda i: (i, 0, 0, 0))] * 2
        out_shape += [jax.ShapeDtypeStruct((t // cache_seq, N_HEADS, cache_seq, HEAD_DIM), F32)] * 2
    return pl.pallas_call(
        _qkv_kernel,
        grid=(t // TOKEN_BLOCK,),
        in_specs=[tok, _mod_spec(blocks_per_batch),
                  _const_spec((D_MODEL, 3 * D_MODEL)), _const_spec((1, 3 * D_MODEL))],
        out_specs=out_specs,
        out_shape=out_shape,
        compiler_params=_params("arbitrary"),
        name="qkv",
    )(x, mod, w_qkv, b_qkv)


def _head_mask(rows):
    return lax.broadcasted_iota(jnp.int32, (rows, HEAD_PAIR), 1) >= HEAD_DIM


def _ctx_attn_kernel(q_ref, k_ref, v_ref, o_ref):
    n = q_ref.shape[1]
    second = _head_mask(n)
    for hp in range(N_PAIRS):
        lanes = slice(hp * HEAD_PAIR, (hp + 1) * HEAD_PAIR)
        q = q_ref[0, :, lanes] * (HEAD_DIM ** -0.5)
        k = k_ref[0, :, lanes]
        v = v_ref[0, :, lanes]
        zero = jnp.zeros_like(q)
        q2 = jnp.concatenate([jnp.where(second, zero, q), jnp.where(second, q, zero)], axis=0)
        s = _dot_t(q2, k)
        p = jnp.exp(s - jnp.max(s, axis=-1, keepdims=True))
        l = jnp.sum(p, axis=-1, keepdims=True)
        o2 = _dot(p.astype(BF16), v) / l
        o_ref[0, :, lanes] = jnp.where(second, o2[n:], o2[:n]).astype(o_ref.dtype)


def _ctx_attn(q, k, v):
    b, s, _ = q.shape
    spec = pl.BlockSpec((1, s, D_MODEL), lambda bi: (bi, 0, 0))
    return pl.pallas_call(
        _ctx_attn_kernel,
        grid=(b,),
        in_specs=[spec, spec, spec],
        out_specs=spec,
        out_shape=jax.ShapeDtypeStruct((b, s, D_MODEL), BF16),
        compiler_params=_params("arbitrary"),
        name="ctx_attn",
    )(q, k, v)


def _rpb_kernel(rpb_ref, out_ref):
    h = pl.program_id(0)
    shape = (GRID_W, HEAD_PAIR)
    qcol = lax.broadcasted_iota(jnp.int32, shape, 0)
    lane = lax.broadcasted_iota(jnp.int32, shape, 1)
    second = lane >= GRID_W
    kcol = jnp.where(second, lane - GRID_W, lane)
    off = jnp.clip(kcol - qcol + NA_COLS - 1, 0, 2 * NA_COLS - 2)
    start = jnp.clip(qcol - NA_COLS // 2, 0, GRID_W - NA_COLS)
    visible = jnp.logical_and(kcol >= start, kcol < start + NA_COLS)
    n_off = 2 * NA_COLS - 1
    base = h * ((2 * NA_ROWS - 1) * n_off)
    for ro in range(BIAS_ROWS):
        t = jnp.zeros(shape, F32)
        for dd in range(n_off):
            val = jnp.where(second, rpb_ref[base + (ro + 1) * n_off + dd],
                            rpb_ref[base + ro * n_off + dd])
            t = jnp.where(off == dd, val, t)
        out_ref[0, ro] = jnp.where(visible, t, -jnp.inf)


def _rpb_tables(rpb):
    return pl.pallas_call(
        _rpb_kernel,
        grid=(N_HEADS,),
        in_specs=[pl.BlockSpec(memory_space=pltpu.SMEM)],
        out_specs=pl.BlockSpec((1, BIAS_ROWS, GRID_W, HEAD_PAIR), lambda h: (h, 0, 0, 0)),
        out_shape=jax.ShapeDtypeStruct((N_HEADS, BIAS_ROWS, GRID_W, HEAD_PAIR), F32),
        compiler_params=_params("arbitrary"),
        name="rpb_tables",
    )(rpb.reshape(-1))


def _lat_attn_kernel(q_ref, k_ref, v_ref, kc_in_ref, vc_in_ref, bias_ref, o_ref, s_ref, kc_ref,
                     vc_ref):
    rows = q_ref.shape[1] // GRID_W
    window = NA_ROWS * GRID_W
    second = _head_mask(GRID_W)
    kc_ref[...] = jnp.concatenate([kc_in_ref[0, 0], kc_in_ref[0, 1]], axis=-1).astype(BF16)
    vc_ref[...] = jnp.concatenate([vc_in_ref[0, 0], vc_in_ref[0, 1]], axis=-1).astype(BF16)

    def window_start(i):
        rs = jnp.clip(i - NA_ROWS // 2, 0, rows - NA_ROWS)
        return rs, pl.multiple_of(rs * GRID_W, GRID_W)

    def logits(i, slot):
        rs, k0 = window_start(i)
        ro = rs - i + NA_ROWS - 1
        q = q_ref[0, pl.ds(pl.multiple_of(i * GRID_W, GRID_W), GRID_W), :] * (HEAD_DIM ** -0.5)
        zero = jnp.zeros_like(q)
        q2 = jnp.concatenate([jnp.where(second, zero, q), jnp.where(second, q, zero)], axis=0)
        bias = jnp.concatenate(
            [jnp.concatenate([bias_ref[0, ro + 2 * a], bias_ref[1, ro + 2 * a]], axis=0)
             for a in range(NA_ROWS // 2)], axis=1)
        s_ref[slot, :, :window] = _dot_t(q2, k_ref[0, pl.ds(k0, window), :]) + bias
        s_ref[slot, :, window:] = _dot_t(q2, kc_ref[...])

    def finish(i, slot):
        _, k0 = window_start(i)
        s = s_ref[slot]
        p = jnp.exp(s - jnp.max(s, axis=-1, keepdims=True))
        l = jnp.sum(p, axis=-1, keepdims=True)
        p = p.astype(BF16)
        o2 = (_dot(p[:, :window], v_ref[0, pl.ds(k0, window), :])
              + _dot(p[:, window:], vc_ref[...])) / l
        out = jnp.where(second, o2[GRID_W:], o2[:GRID_W])
        o_ref[0, pl.ds(pl.multiple_of(i * GRID_W, GRID_W), GRID_W), :] = out.astype(o_ref.dtype)

    logits(0, 0)

    def body(j, carry):
        i = 2 * j
        logits(i + 1, 1)
        finish(i, 0)
        logits(jnp.minimum(i + 2, rows - 1), 0)
        finish(i + 1, 1)
        return carry

    lax.fori_loop(0, rows // 2, body, 0, unroll=2)


def _lat_attn(q, k, v, kc, vc, bias):
    b, n, _ = q.shape
    lc = kc.shape[2]
    seq = pl.BlockSpec((1, n, HEAD_PAIR), lambda bi, hp: (bi, 0, hp))
    ctx = pl.BlockSpec((1, 2, lc, HEAD_DIM), lambda bi, hp: (bi, hp, 0, 0))
    return pl.pallas_call(
        _lat_attn_kernel,
        grid=(b, N_PAIRS),
        in_specs=[seq, seq, seq, ctx, ctx,
                  pl.BlockSpec((2, BIAS_ROWS, GRID_W, HEAD_PAIR), lambda bi, hp: (hp, 0, 0, 0))],
        out_specs=seq,
        out_shape=jax.ShapeDtypeStruct((b, n, D_MODEL), BF16),
        scratch_shapes=[pltpu.VMEM((2, 2 * GRID_W, NA_ROWS * GRID_W + lc), F32),
                        pltpu.VMEM((lc, HEAD_PAIR), BF16), pltpu.VMEM((lc, HEAD_PAIR), BF16)],
        compiler_params=_params("arbitrary", "arbitrary"),
        name="lat_attn",
    )(q, k, v, kc, vc, bias)


def _angles(rows, cols, period):
    theta = 2.0 * np.pi * ((rows[:, None] * cols[None, :]) % period) / period
    return np.cos(theta), np.sin(theta)


def _channel_dft():
    idx = np.arange(FOURIER_CH)
    c, s = _angles(idx, idx, FOURIER_CH)
    eye = np.eye(FOURIER_GROUPS)
    scale = FOURIER_CH ** -0.5
    return jnp.asarray(np.kron(eye, c) * scale, F32), jnp.asarray(np.kron(eye, s) * scale, F32)


def _position_tables(n, q):
    pos = np.arange(n)
    cb, sb = _angles(np.arange(q), pos, n)
    scale = n ** -0.5
    b_tables = [jnp.asarray(cb * scale, F32), jnp.asarray(-sb * scale, F32)]
    if q == n:
        return b_tables
    ca, sa = _angles(np.arange(n // q) * q, pos, n)
    return [jnp.asarray(ca, F32), jnp.asarray(-sa, F32)] + b_tables


def _block_diag(w):
    eye = jnp.eye(RNN_HEADS, dtype=w.dtype)
    return jnp.einsum('hij,hk->hikj', w, eye).reshape(D_RNN, D_RNN)


def kernel(x_prompt, x_sample, c, state_lru, cache_k, cache_v, c_ctx, ada_w, ada_b, ln1_g, ln1_b, ln2_g, ln2_b, w1, b1, w2, b2, e_w_in, e_b_in, e_conv_w, e_conv_b, e_w_r, e_b_r, e_w_i, e_b_i, e_lam, e_w_out, e_b_out, o_w_qkv, o_b_qkv, o_rpb, o_w_out, o_b_out):
    bp, sp, _ = x_prompt.shape
    bs, ss, _ = x_sample.shape
    row = lambda v: v.reshape(1, -1)

    cond = jnp.concatenate(
        [c_ctx[None], c, jnp.zeros((MOD_ROWS - 1 - bs, D_MODEL), F32)], axis=0)
    mod = _ada_params(cond, ada_w, ada_b)

    streams = [
        dict(x=x_prompt.reshape(bp * sp, D_MODEL), b=bp, s=sp, bpb=0, rows=slice(0, 1)),
        dict(x=x_sample.reshape(bs * ss, D_MODEL), b=bs, s=ss, bpb=ss // TOKEN_BLOCK,
             rows=slice(1, 1 + bs)),
    ]
    cc, sc = _channel_dft()
    new_lru, new_k, new_v = [], [], []

    for layer in range(DEPTH):
        j = layer // 2
        tail = [row(ln1_g[layer]), row(ln1_b[layer]), w1[layer].astype(BF16), row(b1[layer]),
                w2[layer].astype(BF16), row(b2[layer]), row(ln2_g[layer]), row(ln2_b[layer])]
        if layer % 2 == 0:
            w_in = e_w_in[j].astype(BF16)
            w_out = e_w_out[j].astype(BF16)
            wg = jnp.stack([jnp.concatenate([_block_diag(e_w_r[j, d]), _block_diag(e_w_i[j, d])],
                                            axis=1) for d in range(2)]).astype(BF16)
            bg = jnp.concatenate([e_b_r[j], e_b_i[j]], axis=-1)[:, None, :]
            lam = e_lam[j][:, None, :]
            mix = [w_out[:D_RNN], w_out[D_RNN:], row(e_b_out[j])]
            for st in streams:
                m = mod[layer, st['rows']]
                b, s = st['b'], st['s']
                urg, zc, zs = _even_in(st['x'], m, st['bpb'], w_in, row(e_b_in[j]), cc, sc)
                if st['bpb']:
                    h0 = state_lru[:, j]
                else:
                    h0 = jnp.zeros((b, 2, D_RNN), F32)
                hf, hb, fin = _rglru(urg.reshape(b, s, 2 * D_RNN), h0, e_conv_w[j],
                                     row(e_conv_b[j]), wg, bg, lam)
                yb = _fourier(zc.reshape(b, s, D_FOURIER), zs.reshape(b, s, D_FOURIER))
                acts = [(hf.reshape(b * s, D_RNN), D_RNN, 0), (hb.reshape(b * s, D_RNN), D_RNN, 0),
                        (urg, D_RNN, 1), (yb.reshape(b * s, D_FOURIER), D_FOURIER, 0)]
                st['x'] = _post(True, st['x'], m, st['bpb'], acts, mix, tail)
                if not st['bpb']:
                    new_lru.append(fin)
        else:
            w_qkv = o_w_qkv[j].astype(BF16)
            mix = [o_w_out[j].astype(BF16), row(o_b_out[j])]
            bias = _rpb_tables(o_rpb[j])
            for st in streams:
                m = mod[layer, st['rows']]
                b, s = st['b'], st['s']
                latent = bool(st['bpb'])
                q, k, v, *kv_heads = _qkv(st['x'], m, st['bpb'], w_qkv, row(o_b_qkv[j]),
                                          0 if latent else s)
                shape = (b, s, D_MODEL)
                if latent:
                    o = _lat_attn(q.reshape(shape), k.reshape(shape), v.reshape(shape),
                                  cache_k[:, j], cache_v[:, j], bias)
                else:
                    o = _ctx_attn(q.reshape(shape), k.reshape(shape), v.reshape(shape))
                    new_k.append(kv_heads[0])
                    new_v.append(kv_heads[1])
                st['x'] = _post(False, st['x'], m, st['bpb'], [(o.reshape(b * s, D_MODEL), D_MODEL, 0)],
                                mix, tail)

    return (streams[0]['x'].reshape(bp, sp, D_MODEL), streams[1]['x'].reshape(bs, ss, D_MODEL),
            jnp.stack(new_lru, axis=1), jnp.stack(new_k, axis=1), jnp.stack(new_v, axis=1))
```

```python
import functools

import numpy as np
import jax
import jax.numpy as jnp
from jax import lax
from jax.experimental import pallas as pl
from jax.experimental.pallas import tpu as pltpu

D_MODEL = 1024
DEPTH = 2
GRID_W = 64
N_HEADS = 16
HEAD_DIM = D_MODEL // N_HEADS
D_RNN = D_MODEL // 2
RNN_HEADS = 8
RNN_BLOCK = D_RNN // RNN_HEADS
CONV_W = 4
D_FOURIER = D_MODEL // 2
FOURIER_GROUPS = 4
FOURIER_CH = D_FOURIER // FOURIER_GROUPS
D_IN_EVEN = 2 * D_RNN + D_FOURIER
D_FF = 4 * D_MODEL
NA_ROWS = 8
NA_COLS = 16
C_SCALE = 8.0
ALPHA = (2 * DEPTH) ** 0.25
LN_EPS = 1e-5

F32 = jnp.float32
BF16 = jnp.bfloat16

SUBLANES = 8
VMEM_LIMIT = 56 * 1024 * 1024
TOKEN_BLOCK = 512
SCAN_CHUNK = 256
FF_CHUNK = 1024
POST_SPLIT = 2
FOURIER_SPLIT = 64
HEAD_PAIR = 2 * HEAD_DIM
N_PAIRS = N_HEADS // 2
BIAS_ROWS = 2 * NA_ROWS - 2
MOD_ROWS = 8


def _params(*sem):
    return pltpu.CompilerParams(dimension_semantics=sem, vmem_limit_bytes=VMEM_LIMIT)


def _const_spec(shape):
    zeros = (0,) * len(shape)
    return pl.BlockSpec(shape, lambda *_: zeros, pipeline_mode=pl.Buffered(1))


def _ln(x):
    mu = jnp.mean(x, axis=-1, keepdims=True)
    xc = x - mu
    var = jnp.mean(xc * xc, axis=-1, keepdims=True)
    return xc * lax.rsqrt(var + LN_EPS)


def _dot(a, b):
    return jnp.dot(a, b, preferred_element_type=F32)


def _dot_t(a, b):
    return lax.dot_general(a, b, (((1,), (1,)), ((), ())), preferred_element_type=F32)


def _ada_kernel(cond_ref, w_ref, b_ref, out_ref):
    cnd = cond_ref[...]
    s = cnd * jax.nn.sigmoid(cnd)
    out_ref[0] = jnp.dot(s, w_ref[0], preferred_element_type=F32,
                         precision=lax.Precision.HIGHEST) + b_ref[0]


def _ada_params(cond, ada_w, ada_b):
    out = pl.pallas_call(
        _ada_kernel,
        grid=(DEPTH, 6),
        in_specs=[
            pl.BlockSpec((MOD_ROWS, D_MODEL), lambda l, k: (0, 0)),
            pl.BlockSpec((1, D_MODEL, D_MODEL), lambda l, k: (l, 0, k)),
            pl.BlockSpec((1, 1, D_MODEL), lambda l, k: (l, 0, k)),
        ],
        out_specs=pl.BlockSpec((1, MOD_ROWS, D_MODEL), lambda l, k: (l, 0, k)),
        out_shape=jax.ShapeDtypeStruct((DEPTH, MOD_ROWS, 6 * D_MODEL), F32),
        compiler_params=_params("arbitrary", "arbitrary"),
        name="ada_params",
    )(cond, ada_w, ada_b.reshape(DEPTH, 1, 6 * D_MODEL))
    return out.reshape(DEPTH, MOD_ROWS, 6, D_MODEL)


def _mod_spec(blocks_per_batch):
    if blocks_per_batch:
        return pl.BlockSpec((1, 6, D_MODEL), lambda i: (i // blocks_per_batch, 0, 0))
    return pl.BlockSpec((1, 6, D_MODEL), lambda i: (0, 0, 0))


def _modulate(x, mod_ref, shift_row, scale_row):
    return (_ln(x) * (1.0 + mod_ref[0, scale_row:scale_row + 1, :])
            + mod_ref[0, shift_row:shift_row + 1, :])


def _even_in_kernel(x_ref, mod_ref, w_ref, b_ref, cc_ref, sc_ref, urg_ref, zc_ref, zs_ref):
    h = _modulate(x_ref[...], mod_ref, 0, 1)
    u = _dot(h.astype(BF16), w_ref[...]) + b_ref[...]
    urg_ref[...] = u[:, :2 * D_RNN]
    uf = u[:, 2 * D_RNN:].astype(BF16)
    zc_ref[...] = _dot(uf, cc_ref[...].astype(BF16)).astype(BF16)
    zs_ref[...] = _dot(uf, sc_ref[...].astype(BF16)).astype(BF16)


def _even_in(x, mod, blocks_per_batch, w_in, b_in, cc, sc):
    t = x.shape[0]
    tok = lambda n: pl.BlockSpec((TOKEN_BLOCK, n), lambda i: (i, 0))
    return pl.pallas_call(
        _even_in_kernel,
        grid=(t // TOKEN_BLOCK,),
        in_specs=[tok(D_MODEL), _mod_spec(blocks_per_batch),
                  _const_spec((D_MODEL, D_IN_EVEN)), _const_spec((1, D_IN_EVEN)),
                  _const_spec((D_FOURIER, D_FOURIER)), _const_spec((D_FOURIER, D_FOURIER))],
        out_specs=[tok(2 * D_RNN), tok(D_FOURIER), tok(D_FOURIER)],
        out_shape=[jax.ShapeDtypeStruct((t, 2 * D_RNN), F32),
                   jax.ShapeDtypeStruct((t, D_FOURIER), BF16),
                   jax.ShapeDtypeStruct((t, D_FOURIER), BF16)],
        compiler_params=_params("arbitrary"),
        name="even_in",
    )(x, mod, w_in, b_in, cc, sc)


def _softplus(x):
    return jnp.maximum(x, 0.0) + jnp.log1p(jnp.exp(-jnp.abs(x)))


def _conv(cur, prev8, next8, w_ref, b_ref):
    n = cur.shape[0]
    ext = jnp.concatenate([prev8, cur, next8], axis=0)
    out = b_ref[...]
    for k in range(CONV_W):
        shift = CONV_W // 2 - k
        if shift == 0:
            tap = cur
        else:
            tap = pltpu.roll(ext, shift % ext.shape[0], 0)[SUBLANES:SUBLANES + n]
        out = out + tap * w_ref[k:k + 1, :]
    return out


def _gates(xc, d, wg_ref, bg_ref, lam_ref, a_ref, b_ref):
    g = _dot(xc.astype(BF16), wg_ref[d]) + bg_ref[d]
    r = jax.nn.sigmoid(g[:, :D_RNN])
    i = jax.nn.sigmoid(g[:, D_RNN:])
    log_a = (-C_SCALE * r) * _softplus(-lam_ref[d])
    a = jnp.exp(log_a)
    a_ref[...] = a
    b_ref[...] = jnp.sqrt(-jnp.tanh(log_a) * (a * a + 1.0)) * (i * xc)


def _scan(a_ref, b_ref, out_ref, carry, reverse):
    n_tiles = a_ref.shape[0] // SUBLANES
    row = lax.broadcasted_iota(jnp.int32, (SUBLANES, D_RNN), 0)

    def body(t, h):
        tile = (n_tiles - 1 - t) if reverse else t
        r0 = pl.multiple_of(tile * SUBLANES, SUBLANES)
        a = a_ref[pl.ds(r0, SUBLANES), :]
        b = b_ref[pl.ds(r0, SUBLANES), :]
        for k in (1, 2, 4):
            if reverse:
                shift, valid = SUBLANES - k, row < SUBLANES - k
            else:
                shift, valid = k, row >= k
            a_sh = pltpu.roll(a, shift, 0)
            b_sh = pltpu.roll(b, shift, 0)
            b = jnp.where(valid, a * b_sh + b, b)
            a = jnp.where(valid, a * a_sh, a)
        hh = a * h + b
        out_ref[0, pl.ds(r0, SUBLANES), :] = hh
        return hh[0:1] if reverse else hh[SUBLANES - 1:SUBLANES]

    return lax.fori_loop(0, n_tiles, body, carry)


def _rglru_kernel(multi, *refs):
    if multi:
        (xf_ref, xfp_ref, xfn_ref, xb_ref, xbp_ref, xbn_ref, cw_ref, cb_ref, wg_ref, bg_ref,
         lam_ref, h0_ref, hf_ref, hb_ref, st_ref, af_ref, bf_ref, ab_ref, bb_ref, cf_ref,
         cbk_ref) = refs
    else:
        (xf_ref, cw_ref, cb_ref, wg_ref, bg_ref, lam_ref, h0_ref, hf_ref, hb_ref, st_ref,
         af_ref, bf_ref, ab_ref, bb_ref, cf_ref, cbk_ref) = refs
    j = pl.program_id(1)
    n = pl.num_programs(1)

    @pl.when(j == 0)
    def _():
        cf_ref[...] = h0_ref[0, 0:1, :]
        cbk_ref[...] = h0_ref[0, 1:2, :]

    zeros8 = jnp.zeros((SUBLANES, D_RNN), F32)
    if multi:
        xc_f = _conv(xf_ref[0], jnp.where(j > 0, xfp_ref[0], zeros8),
                     jnp.where(j < n - 1, xfn_ref[0], zeros8), cw_ref, cb_ref)
        xc_b = _conv(xb_ref[0], jnp.where(j < n - 1, xbp_ref[0], zeros8),
                     jnp.where(j > 0, xbn_ref[0], zeros8), cw_ref, cb_ref)
    else:
        xc_f = _conv(xf_ref[0], zeros8, zeros8, cw_ref, cb_ref)
        xc_b = xc_f
    _gates(xc_f, 0, wg_ref, bg_ref, lam_ref, af_ref, bf_ref)
    _gates(xc_b, 1, wg_ref, bg_ref, lam_ref, ab_ref, bb_ref)
    cf = _scan(af_ref, bf_ref, hf_ref, cf_ref[...], reverse=False)
    cb = _scan(ab_ref, bb_ref, hb_ref, cbk_ref[...], reverse=True)
    cf_ref[...] = cf
    cbk_ref[...] = cb
    st_ref[0, 0:1, :] = cf
    st_ref[0, 1:2, :] = cb


def _rglru(urg, h0, conv_w, conv_b, wg, bg, lam):
    b, s, _ = urg.shape
    c = SCAN_CHUNK
    n = s // c
    multi = n > 1
    tiles = c // SUBLANES
    last_tile = s // SUBLANES - 1
    chunk = lambda f: pl.BlockSpec((1, c, D_RNN), lambda bi, j: (bi, f(j), 0))
    halo = lambda f: pl.BlockSpec((1, SUBLANES, D_RNN), lambda bi, j: (bi, f(j), 0))
    fwd = lambda j: j
    bwd = lambda j: n - 1 - j
    if multi:
        x_specs = [chunk(fwd),
                   halo(lambda j: jnp.maximum(j * tiles - 1, 0)),
                   halo(lambda j: jnp.minimum((j + 1) * tiles, last_tile)),
                   chunk(bwd),
                   halo(lambda j: jnp.maximum((n - 1 - j) * tiles - 1, 0)),
                   halo(lambda j: jnp.minimum((n - j) * tiles, last_tile))]
        x_args = [urg] * 6
    else:
        x_specs = [chunk(fwd)]
        x_args = [urg]
    small = lambda shape: pl.BlockSpec(shape, lambda bi, j: (0,) * len(shape))
    return pl.pallas_call(
        functools.partial(_rglru_kernel, multi),
        grid=(b, n),
        in_specs=x_specs + [small((CONV_W, D_RNN)), small((1, D_RNN)),
                            small((2, D_RNN, 2 * D_RNN)), small((2, 1, 2 * D_RNN)),
                            small((2, 1, D_RNN)),
                            pl.BlockSpec((1, 2, D_RNN), lambda bi, j: (bi, 0, 0))],
        out_specs=[chunk(fwd), chunk(bwd),
                   pl.BlockSpec((1, 2, D_RNN), lambda bi, j: (bi, 0, 0))],
        out_shape=[jax.ShapeDtypeStruct((b, s, D_RNN), F32),
                   jax.ShapeDtypeStruct((b, s, D_RNN), F32),
                   jax.ShapeDtypeStruct((b, 2, D_RNN), F32)],
        scratch_shapes=[pltpu.VMEM((c, D_RNN), F32)] * 4 + [pltpu.VMEM((1, D_RNN), F32)] * 2,
        compiler_params=_params("arbitrary", "arbitrary"),
        name="rglru",
    )(*x_args, conv_w, conv_b, wg, bg, lam, h0)


def _fourier_kernel(split, *refs):
    if split:
        ca_ref, nsa_ref, cb_ref, nsb_ref, zc_ref, zs_ref, out_ref, cs_ref, nss_ref = refs
    else:
        cb_ref, nsb_ref, zc_ref, zs_ref, out_ref, cs_ref, nss_ref = refs
    i = pl.program_id(0)

    @pl.when(pl.program_id(1) == 0)
    def _():
        if split:
            q = cb_ref.shape[0]
            cb = cb_ref[...]
            nsb = nsb_ref[...]
            per_block = cs_ref.shape[0] // q
            for hh in range(per_block):
                ca = ca_ref[pl.ds(i * per_block + hh, 1), :]
                nsa = nsa_ref[pl.ds(i * per_block + hh, 1), :]
                cs_ref[hh * q:(hh + 1) * q, :] = (ca * cb - nsa * nsb).astype(BF16)
                nss_ref[hh * q:(hh + 1) * q, :] = (nsa * cb + ca * nsb).astype(BF16)
        else:
            cs_ref[...] = cb_ref[...].astype(BF16)
            nss_ref[...] = nsb_ref[...].astype(BF16)

    out_ref[0] = (_dot(cs_ref[...], zc_ref[0]) + _dot(nss_ref[...], zs_ref[0])).astype(BF16)


def _fourier(zc, zs):
    b, s, _ = zc.shape
    tf = min(s, TOKEN_BLOCK)
    split = s > tf
    tables = _position_tables(s, FOURIER_SPLIT if split else s)
    seq = pl.BlockSpec((1, s, D_FOURIER), lambda t, bi: (bi, 0, 0))
    return pl.pallas_call(
        functools.partial(_fourier_kernel, split),
        grid=(s // tf, b),
        in_specs=[_const_spec(tb.shape) for tb in tables] + [seq, seq],
        out_specs=pl.BlockSpec((1, tf, D_FOURIER), lambda t, bi: (bi, t, 0)),
        out_shape=jax.ShapeDtypeStruct((b, s, D_FOURIER), BF16),
        scratch_shapes=[pltpu.VMEM((tf, s), BF16)] * 2,
        compiler_params=_params("arbitrary", "arbitrary"),
        name="fourier",
    )(*tables, zc, zs)


def _post_kernel(even, *refs):
    if even:
        (x_ref, mod_ref, hf_ref, hb_ref, gate_ref, yb_ref, woa_ref, wob_ref, bo_ref,
         l1g_ref, l1b_ref, w1_ref, b1_ref, w2_ref, b2_ref, l2g_ref, l2b_ref, out_ref) = refs
    else:
        (x_ref, mod_ref, o_ref, wo_ref, bo_ref,
         l1g_ref, l1b_ref, w1_ref, b1_ref, w2_ref, b2_ref, l2g_ref, l2b_ref, out_ref) = refs

    def mix(rows):
        if even:
            ya = (hf_ref[rows, :] + hb_ref[rows, :]) * jax.nn.gelu(gate_ref[rows, :])
            return (_dot(ya.astype(BF16), woa_ref[...]) + _dot(yb_ref[rows, :], wob_ref[...])
                    + bo_ref[...])
        return _dot(o_ref[rows, :], wo_ref[...]) + bo_ref[...]

    def mlp(hm):
        acc = jnp.zeros((hm.shape[0], D_MODEL), F32)
        for c0 in range(0, D_FF, FF_CHUNK):
            t = _dot(hm, w1_ref[:, c0:c0 + FF_CHUNK]) + b1_ref[:, c0:c0 + FF_CHUNK]
            t = jnp.square(jnp.maximum(t, 0.0)).astype(BF16)
            acc = acc + _dot(t, w2_ref[c0:c0 + FF_CHUNK, :])
        return acc + b2_ref[...]

    n = x_ref.shape[0] // POST_SPLIT
    groups = [pl.ds(g * n, n) for g in range(POST_SPLIT)]
    ys = [mix(rows) for rows in groups]
    x1s = [_ln(ALPHA * x_ref[rows, :] + mod_ref[0, 2:3, :] * y) * l1g_ref[...] + l1b_ref[...]
           for rows, y in zip(groups, ys)]
    hms = [_modulate(x1, mod_ref, 3, 4).astype(BF16) for x1 in x1s]
    for rows, x1, hm in zip(groups, x1s, hms):
        out_ref[rows, :] = (_ln(ALPHA * x1 + mod_ref[0, 5:6, :] * mlp(hm)) * l2g_ref[...]
                            + l2b_ref[...])


def _post(even, x, mod, blocks_per_batch, acts, mix_weights, tail_weights):
    t = x.shape[0]
    tok = lambda n, col=0: pl.BlockSpec((TOKEN_BLOCK, n), lambda i: (i, col))
    act_specs = [tok(width, col) for _, width, col in acts]
    consts = list(mix_weights) + list(tail_weights)
    return pl.pallas_call(
        functools.partial(_post_kernel, even),
        grid=(t // TOKEN_BLOCK,),
        in_specs=[tok(D_MODEL), _mod_spec(blocks_per_batch)] + act_specs
                 + [_const_spec(w.shape) for w in consts],
        out_specs=tok(D_MODEL),
        out_shape=jax.ShapeDtypeStruct((t, D_MODEL), F32),
        compiler_params=_params("arbitrary"),
        name="post_even" if even else "post_odd",
    )(x, mod, *[a for a, _, _ in acts], *consts)


def _qkv_kernel(x_ref, mod_ref, w_ref, b_ref, q_ref, k_ref, v_ref, *cache_refs):
    h = _modulate(x_ref[...], mod_ref, 0, 1)
    u = _dot(h.astype(BF16), w_ref[...]) + b_ref[...]
    q_ref[...] = u[:, :D_MODEL].astype(BF16)
    k_ref[...] = u[:, D_MODEL:2 * D_MODEL].astype(BF16)
    v_ref[...] = u[:, 2 * D_MODEL:].astype(BF16)
    for which, ref in enumerate(cache_refs):
        n_seq, _, s, _ = ref.shape
        for b in range(n_seq):
            for hd in range(N_HEADS):
                c0 = (1 + which) * D_MODEL + hd * HEAD_DIM
                ref[b, hd] = u[b * s:(b + 1) * s, c0:c0 + HEAD_DIM]


def _qkv(x, mod, blocks_per_batch, w_qkv, b_qkv, cache_seq):
    t = x.shape[0]
    tok = pl.BlockSpec((TOKEN_BLOCK, D_MODEL), lambda i: (i, 0))
    out_specs = [tok, tok, tok]
    out_shape = [jax.ShapeDtypeStruct((t, D_MODEL), BF16)] * 3
    if cache_seq:
        per_block = TOKEN_BLOCK // cache_seq
        out_specs += [pl.BlockSpec((per_block, N_HEADS, cache_seq, HEAD_DIM),
                                   lambda i: (i, 0, 0, 0))] * 2
        out_shape += [jax.ShapeDtypeStruct((t // cache_seq, N_HEADS, cache_seq, HEAD_DIM), F32)] * 2
    return pl.pallas_call(
        _qkv_kernel,
        grid=(t // TOKEN_BLOCK,),
        in_specs=[tok, _mod_spec(blocks_per_batch),
                  _const_spec((D_MODEL, 3 * D_MODEL)), _const_spec((1, 3 * D_MODEL))],
        out_specs=out_specs,
        out_shape=out_shape,
        compiler_params=_params("arbitrary"),
        name="qkv",
    )(x, mod, w_qkv, b_qkv)


def _head_mask(rows):
    return lax.broadcasted_iota(jnp.int32, (rows, HEAD_PAIR), 1) >= HEAD_DIM


def _ctx_attn_kernel(q_ref, k_ref, v_ref, o_ref):
    n = q_ref.shape[1]
    second = _head_mask(n)
    for hp in range(N_PAIRS):
        lanes = slice(hp * HEAD_PAIR, (hp + 1) * HEAD_PAIR)
        q = q_ref[0, :, lanes] * (HEAD_DIM ** -0.5)
        k = k_ref[0, :, lanes]
        v = v_ref[0, :, lanes]
        zero = jnp.zeros_like(q)
        q2 = jnp.concatenate([jnp.where(second, zero, q), jnp.where(second, q, zero)], axis=0)
        s = _dot_t(q2, k)
        p = jnp.exp(s - jnp.max(s, axis=-1, keepdims=True))
        l = jnp.sum(p, axis=-1, keepdims=True)
        o2 = _dot(p.astype(BF16), v) / l
        o_ref[0, :, lanes] = jnp.where(second, o2[n:], o2[:n]).astype(o_ref.dtype)


def _ctx_attn(q, k, v):
    b, s, _ = q.shape
    spec = pl.BlockSpec((1, s, D_MODEL), lambda bi: (bi, 0, 0))
    return pl.pallas_call(
        _ctx_attn_kernel,
        grid=(b,),
        in_specs=[spec, spec, spec],
        out_specs=spec,
        out_shape=jax.ShapeDtypeStruct((b, s, D_MODEL), BF16),
        compiler_params=_params("arbitrary"),
        name="ctx_attn",
    )(q, k, v)


def _rpb_kernel(rpb_ref, out_ref):
    h = pl.program_id(0)
    shape = (GRID_W, HEAD_PAIR)
    qcol = lax.broadcasted_iota(jnp.int32, shape, 0)
    lane = lax.broadcasted_iota(jnp.int32, shape, 1)
    second = lane >= GRID_W
    kcol = jnp.where(second, lane - GRID_W, lane)
    off = jnp.clip(kcol - qcol + NA_COLS - 1, 0, 2 * NA_COLS - 2)
    start = jnp.clip(qcol - NA_COLS // 2, 0, GRID_W - NA_COLS)
    visible = jnp.logical_and(kcol >= start, kcol < start + NA_COLS)
    n_off = 2 * NA_COLS - 1
    base = h * ((2 * NA_ROWS - 1) * n_off)
    for ro in range(BIAS_ROWS):
        t = jnp.zeros(shape, F32)
        for dd in range(n_off):
            val = jnp.where(second, rpb_ref[base + (ro + 1) * n_off + dd],
                            rpb_ref[base + ro * n_off + dd])
            t = jnp.where(off == dd, val, t)
        out_ref[0, ro] = jnp.where(visible, t, -jnp.inf)


def _rpb_tables(rpb):
    return pl.pallas_call(
        _rpb_kernel,
        grid=(N_HEADS,),
        in_specs=[pl.BlockSpec(memory_space=pltpu.SMEM)],
        out_specs=pl.BlockSpec((1, BIAS_ROWS, GRID_W, HEAD_PAIR), lambda h: (h, 0, 0, 0)),
        out_shape=jax.ShapeDtypeStruct((N_HEADS, BIAS_ROWS, GRID_W, HEAD_PAIR), F32),
        compiler_params=_params("arbitrary"),
        name="rpb_tables",
    )(rpb.reshape(-1))


def _lat_attn_kernel(q_ref, k_ref, v_ref, kc_in_ref, vc_in_ref, bias_ref, o_ref, s_ref, kc_ref,
                     vc_ref):
    rows = q_ref.shape[1] // GRID_W
    window = NA_ROWS * GRID_W
    second = _head_mask(GRID_W)
    kc_ref[...] = jnp.concatenate([kc_in_ref[0, 0], kc_in_ref[0, 1]], axis=-1).astype(BF16)
    vc_ref[...] = jnp.concatenate([vc_in_ref[0, 0], vc_in_ref[0, 1]], axis=-1).astype(BF16)

    def window_start(i):
        rs = jnp.clip(i - NA_ROWS // 2, 0, rows - NA_ROWS)
        return rs, pl.multiple_of(rs * GRID_W, GRID_W)

    def logits(i, slot):
        rs, k0 = window_start(i)
        ro = rs - i + NA_ROWS - 1
        q = q_ref[0, pl.ds(pl.multiple_of(i * GRID_W, GRID_W), GRID_W), :] * (HEAD_DIM ** -0.5)
        zero = jnp.zeros_like(q)
        q2 = jnp.concatenate([jnp.where(second, zero, q), jnp.where(second, q, zero)], axis=0)
        bias = jnp.concatenate(
            [jnp.concatenate([bias_ref[0, ro + 2 * a], bias_ref[1, ro + 2 * a]], axis=0)
             for a in range(NA_ROWS // 2)], axis=1)
        s_ref[slot, :, :window] = _dot_t(q2, k_ref[0, pl.ds(k0, window), :]) + bias
        s_ref[slot, :, window:] = _dot_t(q2, kc_ref[...])

    def finish(i, slot):
        _, k0 = window_start(i)
        s = s_ref[slot]
        p = jnp.exp(s - jnp.max(s, axis=-1, keepdims=True))
        l = jnp.sum(p, axis=-1, keepdims=True)
        p = p.astype(BF16)
        o2 = (_dot(p[:, :window], v_ref[0, pl.ds(k0, window), :])
              + _dot(p[:, window:], vc_ref[...])) / l
        out = jnp.where(second, o2[GRID_W:], o2[:GRID_W])
        o_ref[0, pl.ds(pl.multiple_of(i * GRID_W, GRID_W), GRID_W), :] = out.astype(o_ref.dtype)

    logits(0, 0)

    def body(j, carry):
        i = 2 * j
        logits(i + 1, 1)
        finish(i, 0)
        logits(jnp.minimum(i + 2, rows - 1), 0)
        finish(i + 1, 1)
        return carry

    lax.fori_loop(0, rows // 2, body, 0, unroll=4)


def _lat_attn(q, k, v, kc, vc, bias):
    b, n, _ = q.shape
    lc = kc.shape[2]
    seq = pl.BlockSpec((1, n, HEAD_PAIR), lambda bi, hp: (bi, 0, hp))
    ctx = pl.BlockSpec((1, 2, lc, HEAD_DIM), lambda bi, hp: (bi, hp, 0, 0))
    return pl.pallas_call(
        _lat_attn_kernel,
        grid=(b, N_PAIRS),
        in_specs=[seq, seq, seq, ctx, ctx,
                  pl.BlockSpec((2, BIAS_ROWS, GRID_W, HEAD_PAIR), lambda bi, hp: (hp, 0, 0, 0))],
        out_specs=seq,
        out_shape=jax.ShapeDtypeStruct((b, n, D_MODEL), BF16),
        scratch_shapes=[pltpu.VMEM((2, 2 * GRID_W, NA_ROWS * GRID_W + lc), F32),
                        pltpu.VMEM((lc, HEAD_PAIR), BF16), pltpu.VMEM((lc, HEAD_PAIR), BF16)],
        compiler_params=_params("arbitrary", "arbitrary"),
        name="lat_attn",
    )(q, k, v, kc, vc, bias)


def _angles(rows, cols, period):
    theta = 2.0 * np.pi * ((rows[:, None] * cols[None, :]) % period) / period
    return np.cos(theta), np.sin(theta)


def _channel_dft():
    idx = np.arange(FOURIER_CH)
    c, s = _angles(idx, idx, FOURIER_CH)
    eye = np.eye(FOURIER_GROUPS)
    scale = FOURIER_CH ** -0.5
    return jnp.asarray(np.kron(eye, c) * scale, F32), jnp.asarray(np.kron(eye, s) * scale, F32)


def _position_tables(n, q):
    pos = np.arange(n)
    cb, sb = _angles(np.arange(q), pos, n)
    scale = n ** -0.5
    b_tables = [jnp.asarray(cb * scale, F32), jnp.asarray(-sb * scale, F32)]
    if q == n:
        return b_tables
    ca, sa = _angles(np.arange(n // q) * q, pos, n)
    return [jnp.asarray(ca, F32), jnp.asarray(-sa, F32)] + b_tables


def _block_diag(w):
    eye = jnp.eye(RNN_HEADS, dtype=w.dtype)
    return jnp.einsum('hij,hk->hikj', w, eye).reshape(D_RNN, D_RNN)


def kernel(x_prompt, x_sample, c, state_lru, cache_k, cache_v, c_ctx, ada_w, ada_b, ln1_g, ln1_b, ln2_g, ln2_b, w1, b1, w2, b2, e_w_in, e_b_in, e_conv_w, e_conv_b, e_w_r, e_b_r, e_w_i, e_b_i, e_lam, e_w_out, e_b_out, o_w_qkv, o_b_qkv, o_rpb, o_w_out, o_b_out):
    bp, sp, _ = x_prompt.shape
    bs, ss, _ = x_sample.shape
    row = lambda v: v.reshape(1, -1)

    cond = jnp.concatenate(
        [c_ctx[None], c, jnp.zeros((MOD_ROWS - 1 - bs, D_MODEL), F32)], axis=0)
    mod = _ada_params(cond, ada_w, ada_b)

    streams = [
        dict(x=x_prompt.reshape(bp * sp, D_MODEL), b=bp, s=sp, bpb=0, rows=slice(0, 1)),
        dict(x=x_sample.reshape(bs * ss, D_MODEL), b=bs, s=ss, bpb=ss // TOKEN_BLOCK,
             rows=slice(1, 1 + bs)),
    ]
    cc, sc = _channel_dft()
    new_lru, new_k, new_v = [], [], []

    for layer in range(DEPTH):
        j = layer // 2
        tail = [row(ln1_g[layer]), row(ln1_b[layer]), w1[layer].astype(BF16), row(b1[layer]),
                w2[layer].astype(BF16), row(b2[layer]), row(ln2_g[layer]), row(ln2_b[layer])]
        if layer % 2 == 0:
            w_in = e_w_in[j].astype(BF16)
            w_out = e_w_out[j].astype(BF16)
            wg = jnp.stack([jnp.concatenate([_block_diag(e_w_r[j, d]), _block_diag(e_w_i[j, d])],
                                            axis=1) for d in range(2)]).astype(BF16)
            bg = jnp.concatenate([e_b_r[j], e_b_i[j]], axis=-1)[:, None, :]
            lam = e_lam[j][:, None, :]
            mix = [w_out[:D_RNN], w_out[D_RNN:], row(e_b_out[j])]
            for st in streams:
                m = mod[layer, st['rows']]
                b, s = st['b'], st['s']
                urg, zc, zs = _even_in(st['x'], m, st['bpb'], w_in, row(e_b_in[j]), cc, sc)
                if st['bpb']:
                    h0 = state_lru[:, j]
                else:
                    h0 = jnp.zeros((b, 2, D_RNN), F32)
                hf, hb, fin = _rglru(urg.reshape(b, s, 2 * D_RNN), h0, e_conv_w[j],
                                     row(e_conv_b[j]), wg, bg, lam)
                yb = _fourier(zc.reshape(b, s, D_FOURIER), zs.reshape(b, s, D_FOURIER))
                acts = [(hf.reshape(b * s, D_RNN), D_RNN, 0), (hb.reshape(b * s, D_RNN), D_RNN, 0),
                        (urg, D_RNN, 1), (yb.reshape(b * s, D_FOURIER), D_FOURIER, 0)]
                st['x'] = _post(True, st['x'], m, st['bpb'], acts, mix, tail)
                if not st['bpb']:
                    new_lru.append(fin)
        else:
            w_qkv = o_w_qkv[j].astype(BF16)
            mix = [o_w_out[j].astype(BF16), row(o_b_out[j])]
            bias = _rpb_tables(o_rpb[j])
            for st in streams:
                m = mod[layer, st['rows']]
                b, s = st['b'], st['s']
                latent = bool(st['bpb'])
                q, k, v, *kv_heads = _qkv(st['x'], m, st['bpb'], w_qkv, row(o_b_qkv[j]),
                                          0 if latent else s)
                shape = (b, s, D_MODEL)
                if latent:
                    o = _lat_attn(q.reshape(shape), k.reshape(shape), v.reshape(shape),
                                  cache_k[:, j], cache_v[:, j], bias)
                else:
                    o = _ctx_attn(q.reshape(shape), k.reshape(shape), v.reshape(shape))
                    new_k.append(kv_heads[0])
                    new_v.append(kv_heads[1])
                st['x'] = _post(False, st['x'], m, st['bpb'], [(o.reshape(b * s, D_MODEL), D_MODEL, 0)],
                                mix, tail)

    return (streams[0]['x'].reshape(bp, sp, D_MODEL), streams[1]['x'].reshape(bs, ss, D_MODEL),
            jnp.stack(new_lru, axis=1), jnp.stack(new_k, axis=1), jnp.stack(new_v, axis=1))
```

```python
import functools

import numpy as np
import jax
import jax.numpy as jnp
from jax import lax
from jax.experimental import pallas as pl
from jax.experimental.pallas import tpu as pltpu

D_MODEL = 1024
DEPTH = 2
GRID_W = 64
N_HEADS = 16
HEAD_DIM = D_MODEL // N_HEADS
D_RNN = D_MODEL // 2
RNN_HEADS = 8
RNN_BLOCK = D_RNN // RNN_HEADS
CONV_W = 4
D_FOURIER = D_MODEL // 2
FOURIER_GROUPS = 4
FOURIER_CH = D_FOURIER // FOURIER_GROUPS
D_IN_EVEN = 2 * D_RNN + D_FOURIER
D_FF = 4 * D_MODEL
NA_ROWS = 8
NA_COLS = 16
C_SCALE = 8.0
ALPHA = (2 * DEPTH) ** 0.25
LN_EPS = 1e-5

F32 = jnp.float32
BF16 = jnp.bfloat16

SUBLANES = 8
VMEM_LIMIT = 56 * 1024 * 1024
TOKEN_BLOCK = 512
SCAN_CHUNK = 256
FF_CHUNK = 1024
POST_SPLIT = 2
FOURIER_SPLIT = 64
HEAD_PAIR = 2 * HEAD_DIM
N_PAIRS = N_HEADS // 2
BIAS_ROWS = 2 * NA_ROWS - 2
MOD_ROWS = 8


def _params(*sem):
    return pltpu.CompilerParams(dimension_semantics=sem, vmem_limit_bytes=VMEM_LIMIT)


def _const_spec(shape):
    zeros = (0,) * len(shape)
    return pl.BlockSpec(shape, lambda *_: zeros, pipeline_mode=pl.Buffered(1))


def _ln(x):
    mu = jnp.mean(x, axis=-1, keepdims=True)
    xc = x - mu
    var = jnp.mean(xc * xc, axis=-1, keepdims=True)
    return xc * lax.rsqrt(var + LN_EPS)


def _dot(a, b):
    return jnp.dot(a, b, preferred_element_type=F32)


def _dot_t(a, b):
    return lax.dot_general(a, b, (((1,), (1,)), ((), ())), preferred_element_type=F32)


def _ada_kernel(cond_ref, w_ref, b_ref, out_ref):
    cnd = cond_ref[...]
    s = cnd * jax.nn.sigmoid(cnd)
    out_ref[0] = jnp.dot(s, w_ref[0], preferred_element_type=F32,
                         precision=lax.Precision.HIGHEST) + b_ref[0]


def _ada_params(cond, ada_w, ada_b):
    out = pl.pallas_call(
        _ada_kernel,
        grid=(DEPTH, 6),
        in_specs=[
            pl.BlockSpec((MOD_ROWS, D_MODEL), lambda l, k: (0, 0)),
            pl.BlockSpec((1, D_MODEL, D_MODEL), lambda l, k: (l, 0, k)),
            pl.BlockSpec((1, 1, D_MODEL), lambda l, k: (l, 0, k)),
        ],
        out_specs=pl.BlockSpec((1, MOD_ROWS, D_MODEL), lambda l, k: (l, 0, k)),
        out_shape=jax.ShapeDtypeStruct((DEPTH, MOD_ROWS, 6 * D_MODEL), F32),
        compiler_params=_params("arbitrary", "arbitrary"),
        name="ada_params",
    )(cond, ada_w, ada_b.reshape(DEPTH, 1, 6 * D_MODEL))
    return out.reshape(DEPTH, MOD_ROWS, 6, D_MODEL)


def _mod_spec(blocks_per_batch):
    if blocks_per_batch:
        return pl.BlockSpec((1, 6, D_MODEL), lambda i: (i // blocks_per_batch, 0, 0))
    return pl.BlockSpec((1, 6, D_MODEL), lambda i: (0, 0, 0))


def _modulate(x, mod_ref, shift_row, scale_row):
    return (_ln(x) * (1.0 + mod_ref[0, scale_row:scale_row + 1, :])
            + mod_ref[0, shift_row:shift_row + 1, :])


def _even_in_kernel(x_ref, mod_ref, w_ref, b_ref, cc_ref, sc_ref, urg_ref, zc_ref, zs_ref):
    h = _modulate(x_ref[...], mod_ref, 0, 1)
    u = _dot(h.astype(BF16), w_ref[...]) + b_ref[...]
    urg_ref[...] = u[:, :2 * D_RNN]
    uf = u[:, 2 * D_RNN:].astype(BF16)
    zc_ref[...] = _dot(uf, cc_ref[...].astype(BF16)).astype(BF16)
    zs_ref[...] = _dot(uf, sc_ref[...].astype(BF16)).astype(BF16)


def _even_in(x, mod, blocks_per_batch, w_in, b_in, cc, sc):
    t = x.shape[0]
    tok = lambda n: pl.BlockSpec((TOKEN_BLOCK, n), lambda i: (i, 0))
    return pl.pallas_call(
        _even_in_kernel,
        grid=(t // TOKEN_BLOCK,),
        in_specs=[tok(D_MODEL), _mod_spec(blocks_per_batch),
                  _const_spec((D_MODEL, D_IN_EVEN)), _const_spec((1, D_IN_EVEN)),
                  _const_spec((D_FOURIER, D_FOURIER)), _const_spec((D_FOURIER, D_FOURIER))],
        out_specs=[tok(2 * D_RNN), tok(D_FOURIER), tok(D_FOURIER)],
        out_shape=[jax.ShapeDtypeStruct((t, 2 * D_RNN), F32),
                   jax.ShapeDtypeStruct((t, D_FOURIER), BF16),
                   jax.ShapeDtypeStruct((t, D_FOURIER), BF16)],
        compiler_params=_params("arbitrary"),
        name="even_in",
    )(x, mod, w_in, b_in, cc, sc)


def _softplus(x):
    return jnp.maximum(x, 0.0) + jnp.log1p(jnp.exp(-jnp.abs(x)))


def _conv(cur, prev8, next8, w_ref, b_ref):
    n = cur.shape[0]
    ext = jnp.concatenate([prev8, cur, next8], axis=0)
    out = b_ref[...]
    for k in range(CONV_W):
        shift = CONV_W // 2 - k
        if shift == 0:
            tap = cur
        else:
            tap = pltpu.roll(ext, shift % ext.shape[0], 0)[SUBLANES:SUBLANES + n]
        out = out + tap * w_ref[k:k + 1, :]
    return out


def _gates(xc, d, wg_ref, bg_ref, lam_ref, a_ref, b_ref):
    g = _dot(xc.astype(BF16), wg_ref[d]) + bg_ref[d]
    r = jax.nn.sigmoid(g[:, :D_RNN])
    i = jax.nn.sigmoid(g[:, D_RNN:])
    log_a = (-C_SCALE * r) * _softplus(-lam_ref[d])
    a = jnp.exp(log_a)
    a_ref[...] = a
    b_ref[...] = jnp.sqrt(-jnp.tanh(log_a) * (a * a + 1.0)) * (i * xc)


def _scan(a_ref, b_ref, out_ref, carry, reverse):
    n_tiles = a_ref.shape[0] // SUBLANES
    row = lax.broadcasted_iota(jnp.int32, (SUBLANES, D_RNN), 0)

    def body(t, h):
        tile = (n_tiles - 1 - t) if reverse else t
        r0 = pl.multiple_of(tile * SUBLANES, SUBLANES)
        a = a_ref[pl.ds(r0, SUBLANES), :]
        b = b_ref[pl.ds(r0, SUBLANES), :]
        for k in (1, 2, 4):
            if reverse:
                shift, valid = SUBLANES - k, row < SUBLANES - k
            else:
                shift, valid = k, row >= k
            a_sh = pltpu.roll(a, shift, 0)
            b_sh = pltpu.roll(b, shift, 0)
            b = jnp.where(valid, a * b_sh + b, b)
            a = jnp.where(valid, a * a_sh, a)
        hh = a * h + b
        out_ref[0, pl.ds(r0, SUBLANES), :] = hh
        return hh[0:1] if reverse else hh[SUBLANES - 1:SUBLANES]

    return lax.fori_loop(0, n_tiles, body, carry)


def _rglru_kernel(multi, *refs):
    if multi:
        (xf_ref, xfp_ref, xfn_ref, xb_ref, xbp_ref, xbn_ref, cw_ref, cb_ref, wg_ref, bg_ref,
         lam_ref, h0_ref, hf_ref, hb_ref, st_ref, af_ref, bf_ref, ab_ref, bb_ref, cf_ref,
         cbk_ref) = refs
    else:
        (xf_ref, cw_ref, cb_ref, wg_ref, bg_ref, lam_ref, h0_ref, hf_ref, hb_ref, st_ref,
         af_ref, bf_ref, ab_ref, bb_ref, cf_ref, cbk_ref) = refs
    j = pl.program_id(1)
    n = pl.num_programs(1)

    @pl.when(j == 0)
    def _():
        cf_ref[...] = h0_ref[0, 0:1, :]
        cbk_ref[...] = h0_ref[0, 1:2, :]

    zeros8 = jnp.zeros((SUBLANES, D_RNN), F32)
    if multi:
        xc_f = _conv(xf_ref[0], jnp.where(j > 0, xfp_ref[0], zeros8),
                     jnp.where(j < n - 1, xfn_ref[0], zeros8), cw_ref, cb_ref)
        xc_b = _conv(xb_ref[0], jnp.where(j < n - 1, xbp_ref[0], zeros8),
                     jnp.where(j > 0, xbn_ref[0], zeros8), cw_ref, cb_ref)
    else:
        xc_f = _conv(xf_ref[0], zeros8, zeros8, cw_ref, cb_ref)
        xc_b = xc_f
    _gates(xc_f, 0, wg_ref, bg_ref, lam_ref, af_ref, bf_ref)
    _gates(xc_b, 1, wg_ref, bg_ref, lam_ref, ab_ref, bb_ref)
    cf = _scan(af_ref, bf_ref, hf_ref, cf_ref[...], reverse=False)
    cb = _scan(ab_ref, bb_ref, hb_ref, cbk_ref[...], reverse=True)
    cf_ref[...] = cf
    cbk_ref[...] = cb
    st_ref[0, 0:1, :] = cf
    st_ref[0, 1:2, :] = cb


def _rglru(urg, h0, conv_w, conv_b, wg, bg, lam):
    b, s, _ = urg.shape
    c = SCAN_CHUNK
    n = s // c
    multi = n > 1
    tiles = c // SUBLANES
    last_tile = s // SUBLANES - 1
    chunk = lambda f: pl.BlockSpec((1, c, D_RNN), lambda bi, j: (bi, f(j), 0))
    halo = lambda f: pl.BlockSpec((1, SUBLANES, D_RNN), lambda bi, j: (bi, f(j), 0))
    fwd = lambda j: j
    bwd = lambda j: n - 1 - j
    if multi:
        x_specs = [chunk(fwd),
                   halo(lambda j: jnp.maximum(j * tiles - 1, 0)),
                   halo(lambda j: jnp.minimum((j + 1) * tiles, last_tile)),
                   chunk(bwd),
                   halo(lambda j: jnp.maximum((n - 1 - j) * tiles - 1, 0)),
                   halo(lambda j: jnp.minimum((n - j) * tiles, last_tile))]
        x_args = [urg] * 6
    else:
        x_specs = [chunk(fwd)]
        x_args = [urg]
    small = lambda shape: pl.BlockSpec(shape, lambda bi, j: (0,) * len(shape))
    return pl.pallas_call(
        functools.partial(_rglru_kernel, multi),
        grid=(b, n),
        in_specs=x_specs + [small((CONV_W, D_RNN)), small((1, D_RNN)),
                            small((2, D_RNN, 2 * D_RNN)), small((2, 1, 2 * D_RNN)),
                            small((2, 1, D_RNN)),
                            pl.BlockSpec((1, 2, D_RNN), lambda bi, j: (bi, 0, 0))],
        out_specs=[chunk(fwd), chunk(bwd),
                   pl.BlockSpec((1, 2, D_RNN), lambda bi, j: (bi, 0, 0))],
        out_shape=[jax.ShapeDtypeStruct((b, s, D_RNN), F32),
                   jax.ShapeDtypeStruct((b, s, D_RNN), F32),
                   jax.ShapeDtypeStruct((b, 2, D_RNN), F32)],
        scratch_shapes=[pltpu.VMEM((c, D_RNN), F32)] * 4 + [pltpu.VMEM((1, D_RNN), F32)] * 2,
        compiler_params=_params("arbitrary", "arbitrary"),
        name="rglru",
    )(*x_args, conv_w, conv_b, wg, bg, lam, h0)


def _fourier_kernel(split, *refs):
    if split:
        ca_ref, nsa_ref, cb_ref, nsb_ref, zc_ref, zs_ref, out_ref, cs_ref, nss_ref = refs
    else:
        cb_ref, nsb_ref, zc_ref, zs_ref, out_ref, cs_ref, nss_ref = refs
    i = pl.program_id(0)

    @pl.when(pl.program_id(1) == 0)
    def _():
        if split:
            q = cb_ref.shape[0]
            cb = cb_ref[...]
            nsb = nsb_ref[...]
            per_block = cs_ref.shape[0] // q
            for hh in range(per_block):
                ca = ca_ref[pl.ds(i * per_block + hh, 1), :]
                nsa = nsa_ref[pl.ds(i * per_block + hh, 1), :]
                cs_ref[hh * q:(hh + 1) * q, :] = (ca * cb - nsa * nsb).astype(BF16)
                nss_ref[hh * q:(hh + 1) * q, :] = (nsa * cb + ca * nsb).astype(BF16)
        else:
            cs_ref[...] = cb_ref[...].astype(BF16)
            nss_ref[...] = nsb_ref[...].astype(BF16)

    out_ref[0] = (_dot(cs_ref[...], zc_ref[0]) + _dot(nss_ref[...], zs_ref[0])).astype(BF16)


def _fourier(zc, zs):
    b, s, _ = zc.shape
    tf = min(s, TOKEN_BLOCK)
    split = s > tf
    tables = _position_tables(s, FOURIER_SPLIT if split else s)
    seq = pl.BlockSpec((1, s, D_FOURIER), lambda t, bi: (bi, 0, 0))
    return pl.pallas_call(
        functools.partial(_fourier_kernel, split),
        grid=(s // tf, b),
        in_specs=[_const_spec(tb.shape) for tb in tables] + [seq, seq],
        out_specs=pl.BlockSpec((1, tf, D_FOURIER), lambda t, bi: (bi, t, 0)),
        out_shape=jax.ShapeDtypeStruct((b, s, D_FOURIER), BF16),
        scratch_shapes=[pltpu.VMEM((tf, s), BF16)] * 2,
        compiler_params=_params("arbitrary", "arbitrary"),
        name="fourier",
    )(*tables, zc, zs)


def _post_kernel(even, *refs):
    if even:
        (x_ref, mod_ref, hf_ref, hb_ref, gate_ref, yb_ref, woa_ref, wob_ref, bo_ref,
         l1g_ref, l1b_ref, w1_ref, b1_ref, w2_ref, b2_ref, l2g_ref, l2b_ref, out_ref) = refs
    else:
        (x_ref, mod_ref, o_ref, wo_ref, bo_ref,
         l1g_ref, l1b_ref, w1_ref, b1_ref, w2_ref, b2_ref, l2g_ref, l2b_ref, out_ref) = refs

    def mix(rows):
        if even:
            ya = (hf_ref[rows, :] + hb_ref[rows, :]) * jax.nn.gelu(gate_ref[rows, :])
            return (_dot(ya.astype(BF16), woa_ref[...]) + _dot(yb_ref[rows, :], wob_ref[...])
                    + bo_ref[...])
        return _dot(o_ref[rows, :], wo_ref[...]) + bo_ref[...]

    def mlp(hm):
        acc = jnp.zeros((hm.shape[0], D_MODEL), F32)
        for c0 in range(0, D_FF, FF_CHUNK):
            t = _dot(hm, w1_ref[:, c0:c0 + FF_CHUNK]) + b1_ref[:, c0:c0 + FF_CHUNK]
            t = jnp.square(jnp.maximum(t, 0.0)).astype(BF16)
            acc = acc + _dot(t, w2_ref[c0:c0 + FF_CHUNK, :])
        return acc + b2_ref[...]

    n = x_ref.shape[0] // POST_SPLIT
    groups = [pl.ds(g * n, n) for g in range(POST_SPLIT)]
    ys = [mix(rows) for rows in groups]
    x1s = [_ln(ALPHA * x_ref[rows, :] + mod_ref[0, 2:3, :] * y) * l1g_ref[...] + l1b_ref[...]
           for rows, y in zip(groups, ys)]
    hms = [_modulate(x1, mod_ref, 3, 4).astype(BF16) for x1 in x1s]
    for rows, x1, hm in zip(groups, x1s, hms):
        out_ref[rows, :] = (_ln(ALPHA * x1 + mod_ref[0, 5:6, :] * mlp(hm)) * l2g_ref[...]
                            + l2b_ref[...])


def _post(even, x, mod, blocks_per_batch, acts, mix_weights, tail_weights):
    t = x.shape[0]
    tok = lambda n, col=0: pl.BlockSpec((TOKEN_BLOCK, n), lambda i: (i, col))
    act_specs = [tok(width, col) for _, width, col in acts]
    consts = list(mix_weights) + list(tail_weights)
    return pl.pallas_call(
        functools.partial(_post_kernel, even),
        grid=(t // TOKEN_BLOCK,),
        in_specs=[tok(D_MODEL), _mod_spec(blocks_per_batch)] + act_specs
                 + [_const_spec(w.shape) for w in consts],
        out_specs=tok(D_MODEL),
        out_shape=jax.ShapeDtypeStruct((t, D_MODEL), F32),
        compiler_params=_params("arbitrary"),
        name="post_even" if even else "post_odd",
    )(x, mod, *[a for a, _, _ in acts], *consts)


def _qkv_kernel(x_ref, mod_ref, w_ref, b_ref, q_ref, k_ref, v_ref, *cache_refs):
    h = _modulate(x_ref[...], mod_ref, 0, 1)
    u = _dot(h.astype(BF16), w_ref[...]) + b_ref[...]
    q_ref[...] = u[:, :D_MODEL].astype(BF16)
    k_ref[...] = u[:, D_MODEL:2 * D_MODEL].astype(BF16)
    v_ref[...] = u[:, 2 * D_MODEL:].astype(BF16)
    for which, ref in enumerate(cache_refs):
        n_seq, _, s, _ = ref.shape
        for b in range(n_seq):
            for hd in range(N_HEADS):
                c0 = (1 + which) * D_MODEL + hd * HEAD_DIM
                ref[b, hd] = u[b * s:(b + 1) * s, c0:c0 + HEAD_DIM]


def _qkv(x, mod, blocks_per_batch, w_qkv, b_qkv, cache_seq):
    t = x.shape[0]
    tok = pl.BlockSpec((TOKEN_BLOCK, D_MODEL), lambda i: (i, 0))
    out_specs = [tok, tok, tok]
    out_shape = [jax.ShapeDtypeStruct((t, D_MODEL), BF16)] * 3
    if cache_seq:
        per_block = TOKEN_BLOCK // cache_seq
        out_specs += [pl.BlockSpec((per_block, N_HEADS, cache_seq, HEAD_DIM),
                                   lambda i: (i, 0, 0, 0))] * 2
        out_shape += [jax.ShapeDtypeStruct((t // cache_seq, N_HEADS, cache_seq, HEAD_DIM), F32)] * 2
    return pl.pallas_call(
        _qkv_kernel,
        grid=(t // TOKEN_BLOCK,),
        in_specs=[tok, _mod_spec(blocks_per_batch),
                  _const_spec((D_MODEL, 3 * D_MODEL)), _const_spec((1, 3 * D_MODEL))],
        out_specs=out_specs,
        out_shape=out_shape,
        compiler_params=_params("arbitrary"),
        name="qkv",
    )(x, mod, w_qkv, b_qkv)


def _head_mask(rows):
    return lax.broadcasted_iota(jnp.int32, (rows, HEAD_PAIR), 1) >= HEAD_DIM


def _ctx_attn_kernel(q_ref, k_ref, v_ref, o_ref):
    n = q_ref.shape[1]
    second = _head_mask(n)
    for hp in range(N_PAIRS):
        lanes = slice(hp * HEAD_PAIR, (hp + 1) * HEAD_PAIR)
        q = q_ref[0, :, lanes] * (HEAD_DIM ** -0.5)
        k = k_ref[0, :, lanes]
        v = v_ref[0, :, lanes]
        zero = jnp.zeros_like(q)
        q2 = jnp.concatenate([jnp.where(second, zero, q), jnp.where(second, q, zero)], axis=0)
        s = _dot_t(q2, k)
        p = jnp.exp(s - jnp.max(s, axis=-1, keepdims=True))
        l = jnp.sum(p, axis=-1, keepdims=True)
        o2 = _dot(p.astype(BF16), v) / l
        o_ref[0, :, lanes] = jnp.where(second, o2[n:], o2[:n]).astype(o_ref.dtype)


def _ctx_attn(q, k, v):
    b, s, _ = q.shape
    spec = pl.BlockSpec((1, s, D_MODEL), lambda bi: (bi, 0, 0))
    return pl.pallas_call(
        _ctx_attn_kernel,
        grid=(b,),
        in_specs=[spec, spec, spec],
        out_specs=spec,
        out_shape=jax.ShapeDtypeStruct((b, s, D_MODEL), BF16),
        compiler_params=_params("arbitrary"),
        name="ctx_attn",
    )(q, k, v)


def _rpb_kernel(rpb_ref, out_ref):
    h = pl.program_id(0)
    shape = (GRID_W, HEAD_PAIR)
    qcol = lax.broadcasted_iota(jnp.int32, shape, 0)
    lane = lax.broadcasted_iota(jnp.int32, shape, 1)
    second = lane >= GRID_W
    kcol = jnp.where(second, lane - GRID_W, lane)
    off = jnp.clip(kcol - qcol + NA_COLS - 1, 0, 2 * NA_COLS - 2)
    start = jnp.clip(qcol - NA_COLS // 2, 0, GRID_W - NA_COLS)
    visible = jnp.logical_and(kcol >= start, kcol < start + NA_COLS)
    n_off = 2 * NA_COLS - 1
    base = h * ((2 * NA_ROWS - 1) * n_off)
    for ro in range(BIAS_ROWS):
        t = jnp.zeros(shape, F32)
        for dd in range(n_off):
            val = jnp.where(second, rpb_ref[base + (ro + 1) * n_off + dd],
                            rpb_ref[base + ro * n_off + dd])
            t = jnp.where(off == dd, val, t)
        out_ref[0, ro] = jnp.where(visible, t, -jnp.inf)


def _rpb_tables(rpb):
    return pl.pallas_call(
        _rpb_kernel,
        grid=(N_HEADS,),
        in_specs=[pl.BlockSpec(memory_space=pltpu.SMEM)],
        out_specs=pl.BlockSpec((1, BIAS_ROWS, GRID_W, HEAD_PAIR), lambda h: (h, 0, 0, 0)),
        out_shape=jax.ShapeDtypeStruct((N_HEADS, BIAS_ROWS, GRID_W, HEAD_PAIR), F32),
        compiler_params=_params("arbitrary"),
        name="rpb_tables",
    )(rpb.reshape(-1))


def _lat_attn_kernel(q_ref, k_ref, v_ref, kc_in_ref, vc_in_ref, bias_ref, o_ref, s_ref, kc_ref,
                     vc_ref):
    rows = q_ref.shape[1] // GRID_W
    window = NA_ROWS * GRID_W
    second = _head_mask(GRID_W)
    kc_ref[...] = jnp.concatenate([kc_in_ref[0, 0], kc_in_ref[0, 1]], axis=-1).astype(BF16)
    vc_ref[...] = jnp.concatenate([vc_in_ref[0, 0], vc_in_ref[0, 1]], axis=-1).astype(BF16)

    def window_start(i):
        rs = jnp.clip(i - NA_ROWS // 2, 0, rows - NA_ROWS)
        return rs, pl.multiple_of(rs * GRID_W, GRID_W)

    def logits(i, slot):
        rs, k0 = window_start(i)
        ro = rs - i + NA_ROWS - 1
        q = q_ref[0, pl.ds(pl.multiple_of(i * GRID_W, GRID_W), GRID_W), :] * (HEAD_DIM ** -0.5)
        zero = jnp.zeros_like(q)
        q2 = jnp.concatenate([jnp.where(second, zero, q), jnp.where(second, q, zero)], axis=0)
        bias = jnp.concatenate(
            [jnp.concatenate([bias_ref[0, ro + 2 * a], bias_ref[1, ro + 2 * a]], axis=0)
             for a in range(NA_ROWS // 2)], axis=1)
        s_ref[slot, :, :window] = _dot_t(q2, k_ref[0, pl.ds(k0, window), :]) + bias
        s_ref[slot, :, window:] = _dot_t(q2, kc_ref[...])

    def finish(i, slot):
        _, k0 = window_start(i)
        s = s_ref[slot]
        p = jnp.exp(s - jnp.max(s, axis=-1, keepdims=True))
        l = jnp.sum(p, axis=-1, keepdims=True)
        p = p.astype(BF16)
        o2 = (_dot(p[:, :window], v_ref[0, pl.ds(k0, window), :])
              + _dot(p[:, window:], vc_ref[...])) / l
        out = jnp.where(second, o2[GRID_W:], o2[:GRID_W])
        o_ref[0, pl.ds(pl.multiple_of(i * GRID_W, GRID_W), GRID_W), :] = out.astype(o_ref.dtype)

    logits(0, 0)

    def body(j, carry):
        i = 2 * j
        logits(i + 1, 1)
        finish(i, 0)
        logits(jnp.minimum(i + 2, rows - 1), 0)
        finish(i + 1, 1)
        return carry

    lax.fori_loop(0, rows // 2, body, 0, unroll=4)


def _lat_attn(q, k, v, kc, vc, bias):
    b, n, _ = q.shape
    lc = kc.shape[2]
    seq = pl.BlockSpec((1, n, HEAD_PAIR), lambda bi, hp: (bi, 0, hp))
    ctx = pl.BlockSpec((1, 2, lc, HEAD_DIM), lambda bi, hp: (bi, hp, 0, 0))
    return pl.pallas_call(
        _lat_attn_kernel,
        grid=(b, N_PAIRS),
        in_specs=[seq, seq, seq, ctx, ctx,
                  pl.BlockSpec((2, BIAS_ROWS, GRID_W, HEAD_PAIR), lambda bi, hp: (hp, 0, 0, 0))],
        out_specs=seq,
        out_shape=jax.ShapeDtypeStruct((b, n, D_MODEL), BF16),
        scratch_shapes=[pltpu.VMEM((2, 2 * GRID_W, NA_ROWS * GRID_W + lc), F32),
                        pltpu.VMEM((lc, HEAD_PAIR), BF16), pltpu.VMEM((lc, HEAD_PAIR), BF16)],
        compiler_params=_params("arbitrary", "arbitrary"),
        name="lat_attn",
    )(q, k, v, kc, vc, bias)


def _angles(rows, cols, period):
    theta = 2.0 * np.pi * ((rows[:, None] * cols[None, :]) % period) / period
    return np.cos(theta), np.sin(theta)


def _channel_dft():
    idx = np.arange(FOURIER_CH)
    c, s = _angles(idx, idx, FOURIER_CH)
    eye = np.eye(FOURIER_GROUPS)
    scale = FOURIER_CH ** -0.5
    return jnp.asarray(np.kron(eye, c) * scale, F32), jnp.asarray(np.kron(eye, s) * scale, F32)


def _position_tables(n, q):
    pos = np.arange(n)
    cb, sb = _angles(np.arange(q), pos, n)
    scale = n ** -0.5
    b_tables = [jnp.asarray(cb * scale, F32), jnp.asarray(-sb * scale, F32)]
    if q == n:
        return b_tables
    ca, sa = _angles(np.arange(n // q) * q, pos, n)
    return [jnp.asarray(ca, F32), jnp.asarray(-sa, F32)] + b_tables


def _block_diag(w):
    eye = jnp.eye(RNN_HEADS, dtype=w.dtype)
    return jnp.einsum('hij,hk->hikj', w, eye).reshape(D_RNN, D_RNN)


def kernel(x_prompt, x_sample, c, state_lru, cache_k, cache_v, c_ctx, ada_w, ada_b, ln1_g, ln1_b, ln2_g, ln2_b, w1, b1, w2, b2, e_w_in, e_b_in, e_conv_w, e_conv_b, e_w_r, e_b_r, e_w_i, e_b_i, e_lam, e_w_out, e_b_out, o_w_qkv, o_b_qkv, o_rpb, o_w_out, o_b_out):
    bp, sp, _ = x_prompt.shape
    bs, ss, _ = x_sample.shape
    row = lambda v: v.reshape(1, -1)

    cond = jnp.concatenate(
        [c_ctx[None], c, jnp.zeros((MOD_ROWS - 1 - bs, D_MODEL), F32)], axis=0)
    mod = _ada_params(cond, ada_w, ada_b)

    streams = [
        dict(x=x_prompt.reshape(bp * sp, D_MODEL), b=bp, s=sp, bpb=0, rows=slice(0, 1)),
        dict(x=x_sample.reshape(bs * ss, D_MODEL), b=bs, s=ss, bpb=ss // TOKEN_BLOCK,
             rows=slice(1, 1 + bs)),
    ]
    cc, sc = _channel_dft()
    new_lru, new_k, new_v = [], [], []

    for layer in range(DEPTH):
        j = layer // 2
        tail = [row(ln1_g[layer]), row(ln1_b[layer]), w1[layer].astype(BF16), row(b1[layer]),
                w2[layer].astype(BF16), row(b2[layer]), row(ln2_g[layer]), row(ln2_b[layer])]
        if layer % 2 == 0:
            w_in = e_w_in[j].astype(BF16)
            w_out = e_w_out[j].astype(BF16)
            wg = jnp.stack([jnp.concatenate([_block_diag(e_w_r[j, d]), _block_diag(e_w_i[j, d])],
                                            axis=1) for d in range(2)]).astype(BF16)
            bg = jnp.concatenate([e_b_r[j], e_b_i[j]], axis=-1)[:, None, :]
            lam = e_lam[j][:, None, :]
            mix = [w_out[:D_RNN], w_out[D_RNN:], row(e_b_out[j])]
            for st in streams:
                m = mod[layer, st['rows']]
                b, s = st['b'], st['s']
                urg, zc, zs = _even_in(st['x'], m, st['bpb'], w_in, row(e_b_in[j]), cc, sc)
                if st['bpb']:
                    h0 = state_lru[:, j]
                else:
                    h0 = jnp.zeros((b, 2, D_RNN), F32)
                hf, hb, fin = _rglru(urg.reshape(b, s, 2 * D_RNN), h0, e_conv_w[j],
                                     row(e_conv_b[j]), wg, bg, lam)
                yb = _fourier(zc.reshape(b, s, D_FOURIER), zs.reshape(b, s, D_FOURIER))
                acts = [(hf.reshape(b * s, D_RNN), D_RNN, 0), (hb.reshape(b * s, D_RNN), D_RNN, 0),
                        (urg, D_RNN, 1), (yb.reshape(b * s, D_FOURIER), D_FOURIER, 0)]
                st['x'] = _post(True, st['x'], m, st['bpb'], acts, mix, tail)
                if not st['bpb']:
                    new_lru.append(fin)
        else:
            w_qkv = o_w_qkv[j].astype(BF16)
            mix = [o_w_out[j].astype(BF16), row(o_b_out[j])]
            bias = _rpb_tables(o_rpb[j])
            for st in streams:
                m = mod[layer, st['rows']]
                b, s = st['b'], st['s']
                latent = bool(st['bpb'])
                q, k, v, *kv_heads = _qkv(st['x'], m, st['bpb'], w_qkv, row(o_b_qkv[j]),
                                          0 if latent else s)
                shape = (b, s, D_MODEL)
                if latent:
                    o = _lat_attn(q.reshape(shape), k.reshape(shape), v.reshape(shape),
                                  cache_k[:, j], cache_v[:, j], bias)
                else:
                    o = _ctx_attn(q.reshape(shape), k.reshape(shape), v.reshape(shape))
                    new_k.append(kv_heads[0])
                    new_v.append(kv_heads[1])
                st['x'] = _post(False, st['x'], m, st['bpb'], [(o.reshape(b * s, D_MODEL), D_MODEL, 0)],
                                mix, tail)

    def stack(parts):
        return parts[0][:, None] if len(parts) == 1 else jnp.stack(parts, axis=1)

    return (streams[0]['x'].reshape(bp, sp, D_MODEL), streams[1]['x'].reshape(bs, ss, D_MODEL),
            stack(new_lru), stack(new_k), stack(new_v))
```

```python
import functools

import numpy as np
import jax
import jax.numpy as jnp
from jax import lax
from jax.experimental import pallas as pl
from jax.experimental.pallas import tpu as pltpu

D_MODEL = 1024
DEPTH = 2
GRID_W = 64
N_HEADS = 16
HEAD_DIM = D_MODEL // N_HEADS
D_RNN = D_MODEL // 2
RNN_HEADS = 8
RNN_BLOCK = D_RNN // RNN_HEADS
CONV_W = 4
D_FOURIER = D_MODEL // 2
FOURIER_GROUPS = 4
FOURIER_CH = D_FOURIER // FOURIER_GROUPS
D_IN_EVEN = 2 * D_RNN + D_FOURIER
D_FF = 4 * D_MODEL
NA_ROWS = 8
NA_COLS = 16
C_SCALE = 8.0
ALPHA = (2 * DEPTH) ** 0.25
LN_EPS = 1e-5
LOG2E = 1.4426950408889634
Q_SCALE = HEAD_DIM ** -0.5 * LOG2E

F32 = jnp.float32
BF16 = jnp.bfloat16

SUBLANES = 8
VMEM_LIMIT = 56 * 1024 * 1024
TOKEN_BLOCK = 512
SCAN_CHUNK = 256
FF_CHUNK = 1024
POST_SPLIT = 2
FOURIER_SPLIT = 64
HEAD_PAIR = 2 * HEAD_DIM
N_PAIRS = N_HEADS // 2
BIAS_ROWS = 2 * NA_ROWS - 2
MOD_ROWS = 8


def _params(*sem):
    return pltpu.CompilerParams(dimension_semantics=sem, vmem_limit_bytes=VMEM_LIMIT)


def _const_spec(shape):
    zeros = (0,) * len(shape)
    return pl.BlockSpec(shape, lambda *_: zeros, pipeline_mode=pl.Buffered(1))


def _ln(x):
    mu = jnp.mean(x, axis=-1, keepdims=True)
    xc = x - mu
    var = jnp.mean(xc * xc, axis=-1, keepdims=True)
    return xc * lax.rsqrt(var + LN_EPS)


def _dot(a, b):
    return jnp.dot(a, b, preferred_element_type=F32)


def _dot_t(a, b):
    return lax.dot_general(a, b, (((1,), (1,)), ((), ())), preferred_element_type=F32)


def _ada_kernel(cond_ref, w_ref, b_ref, out_ref):
    cnd = cond_ref[...]
    s = cnd * jax.nn.sigmoid(cnd)
    out_ref[0] = jnp.dot(s, w_ref[0], preferred_element_type=F32,
                         precision=lax.Precision.HIGHEST) + b_ref[0]


def _ada_params(cond, ada_w, ada_b):
    out = pl.pallas_call(
        _ada_kernel,
        grid=(DEPTH, 6),
        in_specs=[
            pl.BlockSpec((MOD_ROWS, D_MODEL), lambda l, k: (0, 0)),
            pl.BlockSpec((1, D_MODEL, D_MODEL), lambda l, k: (l, 0, k)),
            pl.BlockSpec((1, 1, D_MODEL), lambda l, k: (l, 0, k)),
        ],
        out_specs=pl.BlockSpec((1, MOD_ROWS, D_MODEL), lambda l, k: (l, 0, k)),
        out_shape=jax.ShapeDtypeStruct((DEPTH, MOD_ROWS, 6 * D_MODEL), F32),
        compiler_params=_params("arbitrary", "arbitrary"),
        name="ada_params",
    )(cond, ada_w, ada_b.reshape(DEPTH, 1, 6 * D_MODEL))
    return out.reshape(DEPTH, MOD_ROWS, 6, D_MODEL)


def _mod_spec(blocks_per_batch):
    if blocks_per_batch:
        return pl.BlockSpec((1, 6, D_MODEL), lambda i: (i // blocks_per_batch, 0, 0))
    return pl.BlockSpec((1, 6, D_MODEL), lambda i: (0, 0, 0))


def _modulate(x, mod_ref, shift_row, scale_row):
    return (_ln(x) * (1.0 + mod_ref[0, scale_row:scale_row + 1, :])
            + mod_ref[0, shift_row:shift_row + 1, :])


def _even_in_kernel(x_ref, mod_ref, w_ref, b_ref, cc_ref, sc_ref, urg_ref, zc_ref, zs_ref):
    h = _modulate(x_ref[...], mod_ref, 0, 1)
    u = _dot(h.astype(BF16), w_ref[...]) + b_ref[...]
    urg_ref[...] = u[:, :2 * D_RNN]
    uf = u[:, 2 * D_RNN:].astype(BF16)
    zc_ref[...] = _dot(uf, cc_ref[...].astype(BF16)).astype(BF16)
    zs_ref[...] = _dot(uf, sc_ref[...].astype(BF16)).astype(BF16)


def _even_in(x, mod, blocks_per_batch, w_in, b_in, cc, sc):
    t = x.shape[0]
    tok = lambda n: pl.BlockSpec((TOKEN_BLOCK, n), lambda i: (i, 0))
    return pl.pallas_call(
        _even_in_kernel,
        grid=(t // TOKEN_BLOCK,),
        in_specs=[tok(D_MODEL), _mod_spec(blocks_per_batch),
                  _const_spec((D_MODEL, D_IN_EVEN)), _const_spec((1, D_IN_EVEN)),
                  _const_spec((D_FOURIER, D_FOURIER)), _const_spec((D_FOURIER, D_FOURIER))],
        out_specs=[tok(2 * D_RNN), tok(D_FOURIER), tok(D_FOURIER)],
        out_shape=[jax.ShapeDtypeStruct((t, 2 * D_RNN), F32),
                   jax.ShapeDtypeStruct((t, D_FOURIER), BF16),
                   jax.ShapeDtypeStruct((t, D_FOURIER), BF16)],
        compiler_params=_params("arbitrary"),
        name="even_in",
    )(x, mod, w_in, b_in, cc, sc)


def _softplus(x):
    return jnp.maximum(x, 0.0) + jnp.log1p(jnp.exp(-jnp.abs(x)))


def _conv(cur, prev8, next8, w_ref, b_ref):
    n = cur.shape[0]
    ext = jnp.concatenate([prev8, cur, next8], axis=0)
    out = b_ref[...]
    for k in range(CONV_W):
        shift = CONV_W // 2 - k
        if shift == 0:
            tap = cur
        else:
            tap = pltpu.roll(ext, shift % ext.shape[0], 0)[SUBLANES:SUBLANES + n]
        out = out + tap * w_ref[k:k + 1, :]
    return out


def _gates(xc, d, wg_ref, bg_ref, lam_ref, a_ref, b_ref):
    g = _dot(xc.astype(BF16), wg_ref[d]) + bg_ref[d]
    r = jax.nn.sigmoid(g[:, :D_RNN])
    i = jax.nn.sigmoid(g[:, D_RNN:])
    log_a = (-C_SCALE * r) * _softplus(-lam_ref[d])
    a = jnp.exp(log_a)
    a_ref[...] = a
    b_ref[...] = jnp.sqrt(-jnp.tanh(log_a) * (a * a + 1.0)) * (i * xc)


def _scan(a_ref, b_ref, out_ref, carry, reverse):
    n_tiles = a_ref.shape[0] // SUBLANES
    row = lax.broadcasted_iota(jnp.int32, (SUBLANES, D_RNN), 0)

    def body(t, h):
        tile = (n_tiles - 1 - t) if reverse else t
        r0 = pl.multiple_of(tile * SUBLANES, SUBLANES)
        a = a_ref[pl.ds(r0, SUBLANES), :]
        b = b_ref[pl.ds(r0, SUBLANES), :]
        for k in (1, 2, 4):
            if reverse:
                shift, valid = SUBLANES - k, row < SUBLANES - k
            else:
                shift, valid = k, row >= k
            a_sh = pltpu.roll(a, shift, 0)
            b_sh = pltpu.roll(b, shift, 0)
            b = jnp.where(valid, a * b_sh + b, b)
            a = jnp.where(valid, a * a_sh, a)
        hh = a * h + b
        out_ref[0, pl.ds(r0, SUBLANES), :] = hh
        return hh[0:1] if reverse else hh[SUBLANES - 1:SUBLANES]

    return lax.fori_loop(0, n_tiles, body, carry)


def _rglru_kernel(multi, *refs):
    if multi:
        (xf_ref, xfp_ref, xfn_ref, xb_ref, xbp_ref, xbn_ref, cw_ref, cb_ref, wg_ref, bg_ref,
         lam_ref, h0_ref, hf_ref, hb_ref, st_ref, af_ref, bf_ref, ab_ref, bb_ref, cf_ref,
         cbk_ref) = refs
    else:
        (xf_ref, cw_ref, cb_ref, wg_ref, bg_ref, lam_ref, h0_ref, hf_ref, hb_ref, st_ref,
         af_ref, bf_ref, ab_ref, bb_ref, cf_ref, cbk_ref) = refs
    j = pl.program_id(1)
    n = pl.num_programs(1)

    @pl.when(j == 0)
    def _():
        cf_ref[...] = h0_ref[0, 0:1, :]
        cbk_ref[...] = h0_ref[0, 1:2, :]

    zeros8 = jnp.zeros((SUBLANES, D_RNN), F32)
    if multi:
        xc_f = _conv(xf_ref[0], jnp.where(j > 0, xfp_ref[0], zeros8),
                     jnp.where(j < n - 1, xfn_ref[0], zeros8), cw_ref, cb_ref)
        xc_b = _conv(xb_ref[0], jnp.where(j < n - 1, xbp_ref[0], zeros8),
                     jnp.where(j > 0, xbn_ref[0], zeros8), cw_ref, cb_ref)
    else:
        xc_f = _conv(xf_ref[0], zeros8, zeros8, cw_ref, cb_ref)
        xc_b = xc_f
    _gates(xc_f, 0, wg_ref, bg_ref, lam_ref, af_ref, bf_ref)
    _gates(xc_b, 1, wg_ref, bg_ref, lam_ref, ab_ref, bb_ref)
    cf = _scan(af_ref, bf_ref, hf_ref, cf_ref[...], reverse=False)
    cb = _scan(ab_ref, bb_ref, hb_ref, cbk_ref[...], reverse=True)
    cf_ref[...] = cf
    cbk_ref[...] = cb
    st_ref[0, 0:1, :] = cf
    st_ref[0, 1:2, :] = cb


def _rglru(urg, h0, conv_w, conv_b, wg, bg, lam):
    b, s, _ = urg.shape
    c = SCAN_CHUNK
    n = s // c
    multi = n > 1
    tiles = c // SUBLANES
    last_tile = s // SUBLANES - 1
    chunk = lambda f: pl.BlockSpec((1, c, D_RNN), lambda bi, j: (bi, f(j), 0))
    halo = lambda f: pl.BlockSpec((1, SUBLANES, D_RNN), lambda bi, j: (bi, f(j), 0))
    fwd = lambda j: j
    bwd = lambda j: n - 1 - j
    if multi:
        x_specs = [chunk(fwd),
                   halo(lambda j: jnp.maximum(j * tiles - 1, 0)),
                   halo(lambda j: jnp.minimum((j + 1) * tiles, last_tile)),
                   chunk(bwd),
                   halo(lambda j: jnp.maximum((n - 1 - j) * tiles - 1, 0)),
                   halo(lambda j: jnp.minimum((n - j) * tiles, last_tile))]
        x_args = [urg] * 6
    else:
        x_specs = [chunk(fwd)]
        x_args = [urg]
    small = lambda shape: pl.BlockSpec(shape, lambda bi, j: (0,) * len(shape))
    return pl.pallas_call(
        functools.partial(_rglru_kernel, multi),
        grid=(b, n),
        in_specs=x_specs + [small((CONV_W, D_RNN)), small((1, D_RNN)),
                            small((2, D_RNN, 2 * D_RNN)), small((2, 1, 2 * D_RNN)),
                            small((2, 1, D_RNN)),
                            pl.BlockSpec((1, 2, D_RNN), lambda bi, j: (bi, 0, 0))],
        out_specs=[chunk(fwd), chunk(bwd),
                   pl.BlockSpec((1, 2, D_RNN), lambda bi, j: (bi, 0, 0))],
        out_shape=[jax.ShapeDtypeStruct((b, s, D_RNN), F32),
                   jax.ShapeDtypeStruct((b, s, D_RNN), F32),
                   jax.ShapeDtypeStruct((b, 2, D_RNN), F32)],
        scratch_shapes=[pltpu.VMEM((c, D_RNN), F32)] * 4 + [pltpu.VMEM((1, D_RNN), F32)] * 2,
        compiler_params=_params("arbitrary", "arbitrary"),
        name="rglru",
    )(*x_args, conv_w, conv_b, wg, bg, lam, h0)


def _fourier_kernel(split, *refs):
    if split:
        ca_ref, nsa_ref, cb_ref, nsb_ref, zc_ref, zs_ref, out_ref, cs_ref, nss_ref = refs
    else:
        cb_ref, nsb_ref, zc_ref, zs_ref, out_ref, cs_ref, nss_ref = refs
    i = pl.program_id(0)

    @pl.when(pl.program_id(1) == 0)
    def _():
        if split:
            q = cb_ref.shape[0]
            cb = cb_ref[...]
            nsb = nsb_ref[...]
            per_block = cs_ref.shape[0] // q
            for hh in range(per_block):
                ca = ca_ref[pl.ds(i * per_block + hh, 1), :]
                nsa = nsa_ref[pl.ds(i * per_block + hh, 1), :]
                cs_ref[hh * q:(hh + 1) * q, :] = (ca * cb - nsa * nsb).astype(BF16)
                nss_ref[hh * q:(hh + 1) * q, :] = (nsa * cb + ca * nsb).astype(BF16)
        else:
            cs_ref[...] = cb_ref[...].astype(BF16)
            nss_ref[...] = nsb_ref[...].astype(BF16)

    out_ref[0] = (_dot(cs_ref[...], zc_ref[0]) + _dot(nss_ref[...], zs_ref[0])).astype(BF16)


def _fourier(zc, zs):
    b, s, _ = zc.shape
    tf = min(s, TOKEN_BLOCK)
    split = s > tf
    tables = _position_tables(s, FOURIER_SPLIT if split else s)
    seq = pl.BlockSpec((1, s, D_FOURIER), lambda t, bi: (bi, 0, 0))
    return pl.pallas_call(
        functools.partial(_fourier_kernel, split),
        grid=(s // tf, b),
        in_specs=[_const_spec(tb.shape) for tb in tables] + [seq, seq],
        out_specs=pl.BlockSpec((1, tf, D_FOURIER), lambda t, bi: (bi, t, 0)),
        out_shape=jax.ShapeDtypeStruct((b, s, D_FOURIER), BF16),
        scratch_shapes=[pltpu.VMEM((tf, s), BF16)] * 2,
        compiler_params=_params("arbitrary", "arbitrary"),
        name="fourier",
    )(*tables, zc, zs)


def _post_kernel(even, *refs):
    if even:
        (x_ref, mod_ref, hf_ref, hb_ref, gate_ref, yb_ref, woa_ref, wob_ref, bo_ref,
         l1g_ref, l1b_ref, w1_ref, b1_ref, w2_ref, b2_ref, l2g_ref, l2b_ref, out_ref) = refs
    else:
        (x_ref, mod_ref, o_ref, wo_ref, bo_ref,
         l1g_ref, l1b_ref, w1_ref, b1_ref, w2_ref, b2_ref, l2g_ref, l2b_ref, out_ref) = refs

    def mix(rows):
        if even:
            ya = (hf_ref[rows, :] + hb_ref[rows, :]) * jax.nn.gelu(gate_ref[rows, :])
            return (_dot(ya.astype(BF16), woa_ref[...]) + _dot(yb_ref[rows, :], wob_ref[...])
                    + bo_ref[...])
        return _dot(o_ref[rows, :], wo_ref[...]) + bo_ref[...]

    def mlp(hm):
        acc = jnp.zeros((hm.shape[0], D_MODEL), F32)
        for c0 in range(0, D_FF, FF_CHUNK):
            t = _dot(hm, w1_ref[:, c0:c0 + FF_CHUNK]) + b1_ref[:, c0:c0 + FF_CHUNK]
            t = jnp.square(jnp.maximum(t, 0.0)).astype(BF16)
            acc = acc + _dot(t, w2_ref[c0:c0 + FF_CHUNK, :])
        return acc + b2_ref[...]

    n = x_ref.shape[0] // POST_SPLIT
    groups = [pl.ds(g * n, n) for g in range(POST_SPLIT)]
    ys = [mix(rows) for rows in groups]
    x1s = [_ln(ALPHA * x_ref[rows, :] + mod_ref[0, 2:3, :] * y) * l1g_ref[...] + l1b_ref[...]
           for rows, y in zip(groups, ys)]
    hms = [_modulate(x1, mod_ref, 3, 4).astype(BF16) for x1 in x1s]
    for rows, x1, hm in zip(groups, x1s, hms):
        out_ref[rows, :] = (_ln(ALPHA * x1 + mod_ref[0, 5:6, :] * mlp(hm)) * l2g_ref[...]
                            + l2b_ref[...])


def _post(even, x, mod, blocks_per_batch, acts, mix_weights, tail_weights):
    t = x.shape[0]
    tok = lambda n, col=0: pl.BlockSpec((TOKEN_BLOCK, n), lambda i: (i, col))
    act_specs = [tok(width, col) for _, width, col in acts]
    consts = list(mix_weights) + list(tail_weights)
    return pl.pallas_call(
        functools.partial(_post_kernel, even),
        grid=(t // TOKEN_BLOCK,),
        in_specs=[tok(D_MODEL), _mod_spec(blocks_per_batch)] + act_specs
                 + [_const_spec(w.shape) for w in consts],
        out_specs=tok(D_MODEL),
        out_shape=jax.ShapeDtypeStruct((t, D_MODEL), F32),
        compiler_params=_params("arbitrary"),
        name="post_even" if even else "post_odd",
    )(x, mod, *[a for a, _, _ in acts], *consts)


def _qkv_kernel(x_ref, mod_ref, w_ref, b_ref, q_ref, k_ref, v_ref, *cache_refs):
    h = _modulate(x_ref[...], mod_ref, 0, 1)
    u = _dot(h.astype(BF16), w_ref[...]) + b_ref[...]
    q_ref[...] = (u[:, :D_MODEL] * Q_SCALE).astype(BF16)
    k_ref[...] = u[:, D_MODEL:2 * D_MODEL].astype(BF16)
    v_ref[...] = u[:, 2 * D_MODEL:].astype(BF16)
    for which, ref in enumerate(cache_refs):
        n_seq, _, s, _ = ref.shape
        for b in range(n_seq):
            for hd in range(N_HEADS):
                c0 = (1 + which) * D_MODEL + hd * HEAD_DIM
                ref[b, hd] = u[b * s:(b + 1) * s, c0:c0 + HEAD_DIM]


def _qkv(x, mod, blocks_per_batch, w_qkv, b_qkv, cache_seq):
    t = x.shape[0]
    tok = pl.BlockSpec((TOKEN_BLOCK, D_MODEL), lambda i: (i, 0))
    out_specs = [tok, tok, tok]
    out_shape = [jax.ShapeDtypeStruct((t, D_MODEL), BF16)] * 3
    if cache_seq:
        per_block = TOKEN_BLOCK // cache_seq
        out_specs += [pl.BlockSpec((per_block, N_HEADS, cache_seq, HEAD_DIM),
                                   lambda i: (i, 0, 0, 0))] * 2
        out_shape += [jax.ShapeDtypeStruct((t // cache_seq, N_HEADS, cache_seq, HEAD_DIM), F32)] * 2
    return pl.pallas_call(
        _qkv_kernel,
        grid=(t // TOKEN_BLOCK,),
        in_specs=[tok, _mod_spec(blocks_per_batch),
                  _const_spec((D_MODEL, 3 * D_MODEL)), _const_spec((1, 3 * D_MODEL))],
        out_specs=out_specs,
        out_shape=out_shape,
        compiler_params=_params("arbitrary"),
        name="qkv",
    )(x, mod, w_qkv, b_qkv)


def _head_mask(rows):
    return lax.broadcasted_iota(jnp.int32, (rows, HEAD_PAIR), 1) >= HEAD_DIM


def _ctx_attn_kernel(q_ref, k_ref, v_ref, o_ref):
    n = q_ref.shape[1]
    second = _head_mask(n)
    for hp in range(N_PAIRS):
        lanes = slice(hp * HEAD_PAIR, (hp + 1) * HEAD_PAIR)
        q = q_ref[0, :, lanes]
        k = k_ref[0, :, lanes]
        v = v_ref[0, :, lanes]
        zero = jnp.zeros_like(q)
        q2 = jnp.concatenate([jnp.where(second, zero, q), jnp.where(second, q, zero)], axis=0)
        s = _dot_t(q2, k)
        p = jnp.exp2(s - jnp.max(s, axis=-1, keepdims=True))
        l = jnp.sum(p, axis=-1, keepdims=True)
        o2 = _dot(p.astype(BF16), v) / l
        o_ref[0, :, lanes] = jnp.where(second, o2[n:], o2[:n]).astype(o_ref.dtype)


def _ctx_attn(q, k, v):
    b, s, _ = q.shape
    spec = pl.BlockSpec((1, s, D_MODEL), lambda bi: (bi, 0, 0))
    return pl.pallas_call(
        _ctx_attn_kernel,
        grid=(b,),
        in_specs=[spec, spec, spec],
        out_specs=spec,
        out_shape=jax.ShapeDtypeStruct((b, s, D_MODEL), BF16),
        compiler_params=_params("arbitrary"),
        name="ctx_attn",
    )(q, k, v)


def _rpb_kernel(rpb_ref, out_ref):
    h = pl.program_id(0)
    shape = (GRID_W, HEAD_PAIR)
    qcol = lax.broadcasted_iota(jnp.int32, shape, 0)
    lane = lax.broadcasted_iota(jnp.int32, shape, 1)
    second = lane >= GRID_W
    kcol = jnp.where(second, lane - GRID_W, lane)
    off = jnp.clip(kcol - qcol + NA_COLS - 1, 0, 2 * NA_COLS - 2)
    start = jnp.clip(qcol - NA_COLS // 2, 0, GRID_W - NA_COLS)
    visible = jnp.logical_and(kcol >= start, kcol < start + NA_COLS)
    n_off = 2 * NA_COLS - 1
    base = h * ((2 * NA_ROWS - 1) * n_off)
    for ro in range(BIAS_ROWS):
        t = jnp.zeros(shape, F32)
        for dd in range(n_off):
            val = jnp.where(second, rpb_ref[base + (ro + 1) * n_off + dd],
                            rpb_ref[base + ro * n_off + dd])
            t = jnp.where(off == dd, val, t)
        out_ref[0, ro] = jnp.where(visible, t * LOG2E, -jnp.inf)


def _rpb_tables(rpb):
    return pl.pallas_call(
        _rpb_kernel,
        grid=(N_HEADS,),
        in_specs=[pl.BlockSpec(memory_space=pltpu.SMEM)],
        out_specs=pl.BlockSpec((1, BIAS_ROWS, GRID_W, HEAD_PAIR), lambda h: (h, 0, 0, 0)),
        out_shape=jax.ShapeDtypeStruct((N_HEADS, BIAS_ROWS, GRID_W, HEAD_PAIR), F32),
        compiler_params=_params("arbitrary"),
        name="rpb_tables",
    )(rpb.reshape(-1))


def _lat_attn_kernel(q_ref, k_ref, v_ref, kc_in_ref, vc_in_ref, bias_ref, o_ref, s_ref, kc_ref,
                     vc_ref):
    rows = q_ref.shape[1] // GRID_W
    window = NA_ROWS * GRID_W
    second = _head_mask(GRID_W)
    kc_ref[...] = jnp.concatenate([kc_in_ref[0, 0], kc_in_ref[0, 1]], axis=-1).astype(BF16)
    vc_ref[...] = jnp.concatenate([vc_in_ref[0, 0], vc_in_ref[0, 1]], axis=-1).astype(BF16)

    def window_start(i):
        rs = jnp.clip(i - NA_ROWS // 2, 0, rows - NA_ROWS)
        return rs, pl.multiple_of(rs * GRID_W, GRID_W)

    def logits(i, slot):
        rs, k0 = window_start(i)
        ro = rs - i + NA_ROWS - 1
        q = q_ref[0, pl.ds(pl.multiple_of(i * GRID_W, GRID_W), GRID_W), :]
        zero = jnp.zeros_like(q)
        q2 = jnp.concatenate([jnp.where(second, zero, q), jnp.where(second, q, zero)], axis=0)
        bias = jnp.concatenate(
            [jnp.concatenate([bias_ref[0, ro + 2 * a], bias_ref[1, ro + 2 * a]], axis=0)
             for a in range(NA_ROWS // 2)], axis=1)
        s_ref[slot, :, :window] = _dot_t(q2, k_ref[0, pl.ds(k0, window), :]) + bias
        s_ref[slot, :, window:] = _dot_t(q2, kc_ref[...])

    def finish(i, slot):
        _, k0 = window_start(i)
        s = s_ref[slot]
        p = jnp.exp2(s - jnp.max(s, axis=-1, keepdims=True))
        l = jnp.sum(p, axis=-1, keepdims=True)
        p = p.astype(BF16)
        o2 = (_dot(p[:, :window], v_ref[0, pl.ds(k0, window), :])
              + _dot(p[:, window:], vc_ref[...])) / l
        out = jnp.where(second, o2[GRID_W:], o2[:GRID_W])
        o_ref[0, pl.ds(pl.multiple_of(i * GRID_W, GRID_W), GRID_W), :] = out.astype(o_ref.dtype)

    logits(0, 0)

    def body(j, carry):
        i = 2 * j
        logits(i + 1, 1)
        finish(i, 0)
        logits(jnp.minimum(i + 2, rows - 1), 0)
        finish(i + 1, 1)
        return carry

    lax.fori_loop(0, rows // 2, body, 0, unroll=4)


def _lat_attn(q, k, v, kc, vc, bias):
    b, n, _ = q.shape
    lc = kc.shape[2]
    seq = pl.BlockSpec((1, n, HEAD_PAIR), lambda bi, hp: (bi, 0, hp))
    ctx = pl.BlockSpec((1, 2, lc, HEAD_DIM), lambda bi, hp: (bi, hp, 0, 0))
    return pl.pallas_call(
        _lat_attn_kernel,
        grid=(b, N_PAIRS),
        in_specs=[seq, seq, seq, ctx, ctx,
                  pl.BlockSpec((2, BIAS_ROWS, GRID_W, HEAD_PAIR), lambda bi, hp: (hp, 0, 0, 0))],
        out_specs=seq,
        out_shape=jax.ShapeDtypeStruct((b, n, D_MODEL), BF16),
        scratch_shapes=[pltpu.VMEM((2, 2 * GRID_W, NA_ROWS * GRID_W + lc), F32),
                        pltpu.VMEM((lc, HEAD_PAIR), BF16), pltpu.VMEM((lc, HEAD_PAIR), BF16)],
        compiler_params=_params("arbitrary", "arbitrary"),
        name="lat_attn",
    )(q, k, v, kc, vc, bias)


def _angles(rows, cols, period):
    theta = 2.0 * np.pi * ((rows[:, None] * cols[None, :]) % period) / period
    return np.cos(theta), np.sin(theta)


def _channel_dft():
    idx = np.arange(FOURIER_CH)
    c, s = _angles(idx, idx, FOURIER_CH)
    eye = np.eye(FOURIER_GROUPS)
    scale = FOURIER_CH ** -0.5
    return jnp.asarray(np.kron(eye, c) * scale, F32), jnp.asarray(np.kron(eye, s) * scale, F32)


def _position_tables(n, q):
    pos = np.arange(n)
    cb, sb = _angles(np.arange(q), pos, n)
    scale = n ** -0.5
    b_tables = [jnp.asarray(cb * scale, F32), jnp.asarray(-sb * scale, F32)]
    if q == n:
        return b_tables
    ca, sa = _angles(np.arange(n // q) * q, pos, n)
    return [jnp.asarray(ca, F32), jnp.asarray(-sa, F32)] + b_tables


def _block_diag(w):
    eye = jnp.eye(RNN_HEADS, dtype=w.dtype)
    return jnp.einsum('hij,hk->hikj', w, eye).reshape(D_RNN, D_RNN)


def kernel(x_prompt, x_sample, c, state_lru, cache_k, cache_v, c_ctx, ada_w, ada_b, ln1_g, ln1_b, ln2_g, ln2_b, w1, b1, w2, b2, e_w_in, e_b_in, e_conv_w, e_conv_b, e_w_r, e_b_r, e_w_i, e_b_i, e_lam, e_w_out, e_b_out, o_w_qkv, o_b_qkv, o_rpb, o_w_out, o_b_out):
    bp, sp, _ = x_prompt.shape
    bs, ss, _ = x_sample.shape
    row = lambda v: v.reshape(1, -1)

    cond = jnp.concatenate(
        [c_ctx[None], c, jnp.zeros((MOD_ROWS - 1 - bs, D_MODEL), F32)], axis=0)
    mod = _ada_params(cond, ada_w, ada_b)

    streams = [
        dict(x=x_prompt.reshape(bp * sp, D_MODEL), b=bp, s=sp, bpb=0, rows=slice(0, 1)),
        dict(x=x_sample.reshape(bs * ss, D_MODEL), b=bs, s=ss, bpb=ss // TOKEN_BLOCK,
             rows=slice(1, 1 + bs)),
    ]
    cc, sc = _channel_dft()
    new_lru, new_k, new_v = [], [], []

    for layer in range(DEPTH):
        j = layer // 2
        tail = [row(ln1_g[layer]), row(ln1_b[layer]), w1[layer].astype(BF16), row(b1[layer]),
                w2[layer].astype(BF16), row(b2[layer]), row(ln2_g[layer]), row(ln2_b[layer])]
        if layer % 2 == 0:
            w_in = e_w_in[j].astype(BF16)
            w_out = e_w_out[j].astype(BF16)
            wg = jnp.stack([jnp.concatenate([_block_diag(e_w_r[j, d]), _block_diag(e_w_i[j, d])],
                                            axis=1) for d in range(2)]).astype(BF16)
            bg = jnp.concatenate([e_b_r[j], e_b_i[j]], axis=-1)[:, None, :]
            lam = e_lam[j][:, None, :]
            mix = [w_out[:D_RNN], w_out[D_RNN:], row(e_b_out[j])]
            for st in streams:
                m = mod[layer, st['rows']]
                b, s = st['b'], st['s']
                urg, zc, zs = _even_in(st['x'], m, st['bpb'], w_in, row(e_b_in[j]), cc, sc)
                if st['bpb']:
                    h0 = state_lru[:, j]
                else:
                    h0 = jnp.zeros((b, 2, D_RNN), F32)
                hf, hb, fin = _rglru(urg.reshape(b, s, 2 * D_RNN), h0, e_conv_w[j],
                                     row(e_conv_b[j]), wg, bg, lam)
                yb = _fourier(zc.reshape(b, s, D_FOURIER), zs.reshape(b, s, D_FOURIER))
                acts = [(hf.reshape(b * s, D_RNN), D_RNN, 0), (hb.reshape(b * s, D_RNN), D_RNN, 0),
                        (urg, D_RNN, 1), (yb.reshape(b * s, D_FOURIER), D_FOURIER, 0)]
                st['x'] = _post(True, st['x'], m, st['bpb'], acts, mix, tail)
                if not st['bpb']:
                    new_lru.append(fin)
        else:
            w_qkv = o_w_qkv[j].astype(BF16)
            mix = [o_w_out[j].astype(BF16), row(o_b_out[j])]
            bias = _rpb_tables(o_rpb[j])
            for st in streams:
                m = mod[layer, st['rows']]
                b, s = st['b'], st['s']
                latent = bool(st['bpb'])
                q, k, v, *kv_heads = _qkv(st['x'], m, st['bpb'], w_qkv, row(o_b_qkv[j]),
                                          0 if latent else s)
                shape = (b, s, D_MODEL)
                if latent:
                    o = _lat_attn(q.reshape(shape), k.reshape(shape), v.reshape(shape),
                                  cache_k[:, j], cache_v[:, j], bias)
                else:
                    o = _ctx_attn(q.reshape(shape), k.reshape(shape), v.reshape(shape))
                    new_k.append(kv_heads[0])
                    new_v.append(kv_heads[1])
                st['x'] = _post(False, st['x'], m, st['bpb'], [(o.reshape(b * s, D_MODEL), D_MODEL, 0)],
                                mix, tail)

    return (streams[0]['x'].reshape(bp, sp, D_MODEL), streams[1]['x'].reshape(bs, ss, D_MODEL),
            jnp.stack(new_lru, axis=1), jnp.stack(new_k, axis=1), jnp.stack(new_v, axis=1))
```

```python
import functools

import numpy as np
import jax
import jax.numpy as jnp
from jax import lax
from jax.experimental import pallas as pl
from jax.experimental.pallas import tpu as pltpu

D_MODEL = 1024
DEPTH = 2
GRID_W = 64
N_HEADS = 16
HEAD_DIM = D_MODEL // N_HEADS
D_RNN = D_MODEL // 2
RNN_HEADS = 8
RNN_BLOCK = D_RNN // RNN_HEADS
CONV_W = 4
D_FOURIER = D_MODEL // 2
FOURIER_GROUPS = 4
FOURIER_CH = D_FOURIER // FOURIER_GROUPS
D_IN_EVEN = 2 * D_RNN + D_FOURIER
D_FF = 4 * D_MODEL
NA_ROWS = 8
NA_COLS = 16
C_SCALE = 8.0
ALPHA = (2 * DEPTH) ** 0.25
LN_EPS = 1e-5
LOG2E = 1.4426950408889634
Q_SCALE = HEAD_DIM ** -0.5 * LOG2E

F32 = jnp.float32
BF16 = jnp.bfloat16

SUBLANES = 8
VMEM_LIMIT = 56 * 1024 * 1024
TOKEN_BLOCK = 512
SCAN_CHUNK = 256
FF_CHUNK = 1024
POST_SPLIT = 2
FOURIER_SPLIT = 64
HEAD_PAIR = 2 * HEAD_DIM
N_PAIRS = N_HEADS // 2
BIAS_ROWS = 2 * NA_ROWS - 2
MOD_ROWS = 8


def _params(*sem):
    return pltpu.CompilerParams(dimension_semantics=sem, vmem_limit_bytes=VMEM_LIMIT)


def _const_spec(shape):
    zeros = (0,) * len(shape)
    return pl.BlockSpec(shape, lambda *_: zeros, pipeline_mode=pl.Buffered(1))


def _ln(x):
    mu = jnp.mean(x, axis=-1, keepdims=True)
    xc = x - mu
    var = jnp.mean(xc * xc, axis=-1, keepdims=True)
    return xc * lax.rsqrt(var + LN_EPS)


def _dot(a, b):
    return jnp.dot(a, b, preferred_element_type=F32)


def _dot_t(a, b):
    return lax.dot_general(a, b, (((1,), (1,)), ((), ())), preferred_element_type=F32)


def _ada_kernel(cond_ref, w_ref, b_ref, out_ref):
    cnd = cond_ref[...]
    s = cnd * jax.nn.sigmoid(cnd)
    out_ref[0] = jnp.dot(s, w_ref[0], preferred_element_type=F32,
                         precision=lax.Precision.HIGHEST) + b_ref[0]


def _ada_params(cond, ada_w, ada_b):
    out = pl.pallas_call(
        _ada_kernel,
        grid=(DEPTH, 6),
        in_specs=[
            pl.BlockSpec((MOD_ROWS, D_MODEL), lambda l, k: (0, 0)),
            pl.BlockSpec((1, D_MODEL, D_MODEL), lambda l, k: (l, 0, k)),
            pl.BlockSpec((1, 1, D_MODEL), lambda l, k: (l, 0, k)),
        ],
        out_specs=pl.BlockSpec((1, MOD_ROWS, D_MODEL), lambda l, k: (l, 0, k)),
        out_shape=jax.ShapeDtypeStruct((DEPTH, MOD_ROWS, 6 * D_MODEL), F32),
        compiler_params=_params("arbitrary", "arbitrary"),
        name="ada_params",
    )(cond, ada_w, ada_b.reshape(DEPTH, 1, 6 * D_MODEL))
    return out.reshape(DEPTH, MOD_ROWS, 6, D_MODEL)


def _mod_spec(blocks_per_batch):
    if blocks_per_batch:
        return pl.BlockSpec((1, 6, D_MODEL), lambda i: (i // blocks_per_batch, 0, 0))
    return pl.BlockSpec((1, 6, D_MODEL), lambda i: (0, 0, 0))


def _modulate(x, mod_ref, shift_row, scale_row):
    return (_ln(x) * (1.0 + mod_ref[0, scale_row:scale_row + 1, :])
            + mod_ref[0, shift_row:shift_row + 1, :])


def _even_in_kernel(x_ref, mod_ref, w_ref, b_ref, cc_ref, sc_ref, urg_ref, zc_ref, zs_ref):
    h = _modulate(x_ref[...], mod_ref, 0, 1)
    u = _dot(h.astype(BF16), w_ref[...]) + b_ref[...]
    urg_ref[...] = u[:, :2 * D_RNN]
    uf = u[:, 2 * D_RNN:].astype(BF16)
    zc_ref[...] = _dot(uf, cc_ref[...].astype(BF16)).astype(BF16)
    zs_ref[...] = _dot(uf, sc_ref[...].astype(BF16)).astype(BF16)


def _even_in(x, mod, blocks_per_batch, w_in, b_in, cc, sc):
    t = x.shape[0]
    tok = lambda n: pl.BlockSpec((TOKEN_BLOCK, n), lambda i: (i, 0))
    return pl.pallas_call(
        _even_in_kernel,
        grid=(t // TOKEN_BLOCK,),
        in_specs=[tok(D_MODEL), _mod_spec(blocks_per_batch),
                  _const_spec((D_MODEL, D_IN_EVEN)), _const_spec((1, D_IN_EVEN)),
                  _const_spec((D_FOURIER, D_FOURIER)), _const_spec((D_FOURIER, D_FOURIER))],
        out_specs=[tok(2 * D_RNN), tok(D_FOURIER), tok(D_FOURIER)],
        out_shape=[jax.ShapeDtypeStruct((t, 2 * D_RNN), F32),
                   jax.ShapeDtypeStruct((t, D_FOURIER), BF16),
                   jax.ShapeDtypeStruct((t, D_FOURIER), BF16)],
        compiler_params=_params("arbitrary"),
        name="even_in",
    )(x, mod, w_in, b_in, cc, sc)


def _softplus(x):
    return jnp.maximum(x, 0.0) + jnp.log1p(jnp.exp(-jnp.abs(x)))


def _conv(cur, prev8, next8, w_ref, b_ref):
    n = cur.shape[0]
    ext = jnp.concatenate([prev8, cur, next8], axis=0)
    out = b_ref[...]
    for k in range(CONV_W):
        shift = CONV_W // 2 - k
        if shift == 0:
            tap = cur
        else:
            tap = pltpu.roll(ext, shift % ext.shape[0], 0)[SUBLANES:SUBLANES + n]
        out = out + tap * w_ref[k:k + 1, :]
    return out


def _gates(xc, d, wg_ref, bg_ref, lam_ref, a_ref, b_ref):
    g = _dot(xc.astype(BF16), wg_ref[d]) + bg_ref[d]
    r = jax.nn.sigmoid(g[:, :D_RNN])
    i = jax.nn.sigmoid(g[:, D_RNN:])
    log_a = (-C_SCALE * r) * _softplus(-lam_ref[d])
    a = jnp.exp(log_a)
    a_ref[...] = a
    b_ref[...] = jnp.sqrt(-jnp.tanh(log_a) * (a * a + 1.0)) * (i * xc)


def _scan(a_ref, b_ref, out_ref, carry, reverse):
    n_tiles = a_ref.shape[0] // SUBLANES
    row = lax.broadcasted_iota(jnp.int32, (SUBLANES, D_RNN), 0)

    def body(t, h):
        tile = (n_tiles - 1 - t) if reverse else t
        r0 = pl.multiple_of(tile * SUBLANES, SUBLANES)
        a = a_ref[pl.ds(r0, SUBLANES), :]
        b = b_ref[pl.ds(r0, SUBLANES), :]
        for k in (1, 2, 4):
            if reverse:
                shift, valid = SUBLANES - k, row < SUBLANES - k
            else:
                shift, valid = k, row >= k
            a_sh = pltpu.roll(a, shift, 0)
            b_sh = pltpu.roll(b, shift, 0)
            b = jnp.where(valid, a * b_sh + b, b)
            a = jnp.where(valid, a * a_sh, a)
        hh = a * h + b
        out_ref[0, pl.ds(r0, SUBLANES), :] = hh
        return hh[0:1] if reverse else hh[SUBLANES - 1:SUBLANES]

    return lax.fori_loop(0, n_tiles, body, carry)


def _rglru_kernel(multi, *refs):
    if multi:
        (xf_ref, xfp_ref, xfn_ref, xb_ref, xbp_ref, xbn_ref, cw_ref, cb_ref, wg_ref, bg_ref,
         lam_ref, h0_ref, hf_ref, hb_ref, st_ref, af_ref, bf_ref, ab_ref, bb_ref, cf_ref,
         cbk_ref) = refs
    else:
        (xf_ref, cw_ref, cb_ref, wg_ref, bg_ref, lam_ref, h0_ref, hf_ref, hb_ref, st_ref,
         af_ref, bf_ref, ab_ref, bb_ref, cf_ref, cbk_ref) = refs
    j = pl.program_id(1)
    n = pl.num_programs(1)

    @pl.when(j == 0)
    def _():
        cf_ref[...] = h0_ref[0, 0:1, :]
        cbk_ref[...] = h0_ref[0, 1:2, :]

    zeros8 = jnp.zeros((SUBLANES, D_RNN), F32)
    if multi:
        xc_f = _conv(xf_ref[0], jnp.where(j > 0, xfp_ref[0], zeros8),
                     jnp.where(j < n - 1, xfn_ref[0], zeros8), cw_ref, cb_ref)
        xc_b = _conv(xb_ref[0], jnp.where(j < n - 1, xbp_ref[0], zeros8),
                     jnp.where(j > 0, xbn_ref[0], zeros8), cw_ref, cb_ref)
    else:
        xc_f = _conv(xf_ref[0], zeros8, zeros8, cw_ref, cb_ref)
        xc_b = xc_f
    _gates(xc_f, 0, wg_ref, bg_ref, lam_ref, af_ref, bf_ref)
    _gates(xc_b, 1, wg_ref, bg_ref, lam_ref, ab_ref, bb_ref)
    cf = _scan(af_ref, bf_ref, hf_ref, cf_ref[...], reverse=False)
    cb = _scan(ab_ref, bb_ref, hb_ref, cbk_ref[...], reverse=True)
    cf_ref[...] = cf
    cbk_ref[...] = cb
    st_ref[0, 0:1, :] = cf
    st_ref[0, 1:2, :] = cb


def _rglru(urg, h0, conv_w, conv_b, wg, bg, lam):
    b, s, _ = urg.shape
    c = SCAN_CHUNK
    n = s // c
    multi = n > 1
    tiles = c // SUBLANES
    last_tile = s // SUBLANES - 1
    chunk = lambda f: pl.BlockSpec((1, c, D_RNN), lambda bi, j: (bi, f(j), 0))
    halo = lambda f: pl.BlockSpec((1, SUBLANES, D_RNN), lambda bi, j: (bi, f(j), 0))
    fwd = lambda j: j
    bwd = lambda j: n - 1 - j
    if multi:
        x_specs = [chunk(fwd),
                   halo(lambda j: jnp.maximum(j * tiles - 1, 0)),
                   halo(lambda j: jnp.minimum((j + 1) * tiles, last_tile)),
                   chunk(bwd),
                   halo(lambda j: jnp.maximum((n - 1 - j) * tiles - 1, 0)),
                   halo(lambda j: jnp.minimum((n - j) * tiles, last_tile))]
        x_args = [urg] * 6
    else:
        x_specs = [chunk(fwd)]
        x_args = [urg]
    small = lambda shape: pl.BlockSpec(shape, lambda bi, j: (0,) * len(shape))
    return pl.pallas_call(
        functools.partial(_rglru_kernel, multi),
        grid=(b, n),
        in_specs=x_specs + [small((CONV_W, D_RNN)), small((1, D_RNN)),
                            small((2, D_RNN, 2 * D_RNN)), small((2, 1, 2 * D_RNN)),
                            small((2, 1, D_RNN)),
                            pl.BlockSpec((1, 2, D_RNN), lambda bi, j: (bi, 0, 0))],
        out_specs=[chunk(fwd), chunk(bwd),
                   pl.BlockSpec((1, 2, D_RNN), lambda bi, j: (bi, 0, 0))],
        out_shape=[jax.ShapeDtypeStruct((b, s, D_RNN), F32),
                   jax.ShapeDtypeStruct((b, s, D_RNN), F32),
                   jax.ShapeDtypeStruct((b, 2, D_RNN), F32)],
        scratch_shapes=[pltpu.VMEM((c, D_RNN), F32)] * 4 + [pltpu.VMEM((1, D_RNN), F32)] * 2,
        compiler_params=_params("arbitrary", "arbitrary"),
        name="rglru",
    )(*x_args, conv_w, conv_b, wg, bg, lam, h0)


def _fourier_kernel(split, *refs):
    if split:
        ca_ref, nsa_ref, cb_ref, nsb_ref, zc_ref, zs_ref, out_ref, cs_ref, nss_ref = refs
    else:
        cb_ref, nsb_ref, zc_ref, zs_ref, out_ref, cs_ref, nss_ref = refs
    i = pl.program_id(0)

    @pl.when(pl.program_id(1) == 0)
    def _():
        if split:
            q = cb_ref.shape[0]
            cb = cb_ref[...]
            nsb = nsb_ref[...]
            per_block = cs_ref.shape[0] // q
            for hh in range(per_block):
                ca = ca_ref[pl.ds(i * per_block + hh, 1), :]
                nsa = nsa_ref[pl.ds(i * per_block + hh, 1), :]
                cs_ref[hh * q:(hh + 1) * q, :] = (ca * cb - nsa * nsb).astype(BF16)
                nss_ref[hh * q:(hh + 1) * q, :] = (nsa * cb + ca * nsb).astype(BF16)
        else:
            cs_ref[...] = cb_ref[...].astype(BF16)
            nss_ref[...] = nsb_ref[...].astype(BF16)

    out_ref[0] = (_dot(cs_ref[...], zc_ref[0]) + _dot(nss_ref[...], zs_ref[0])).astype(BF16)


def _fourier(zc, zs):
    b, s, _ = zc.shape
    tf = min(s, TOKEN_BLOCK)
    split = s > tf
    tables = _position_tables(s, FOURIER_SPLIT if split else s)
    seq = pl.BlockSpec((1, s, D_FOURIER), lambda t, bi: (bi, 0, 0))
    return pl.pallas_call(
        functools.partial(_fourier_kernel, split),
        grid=(s // tf, b),
        in_specs=[_const_spec(tb.shape) for tb in tables] + [seq, seq],
        out_specs=pl.BlockSpec((1, tf, D_FOURIER), lambda t, bi: (bi, t, 0)),
        out_shape=jax.ShapeDtypeStruct((b, s, D_FOURIER), BF16),
        scratch_shapes=[pltpu.VMEM((tf, s), BF16)] * 2,
        compiler_params=_params("arbitrary", "arbitrary"),
        name="fourier",
    )(*tables, zc, zs)


def _post_kernel(even, *refs):
    if even:
        (x_ref, mod_ref, hf_ref, hb_ref, gate_ref, yb_ref, woa_ref, wob_ref, bo_ref,
         l1g_ref, l1b_ref, w1_ref, b1_ref, w2_ref, b2_ref, l2g_ref, l2b_ref, out_ref) = refs
    else:
        (x_ref, mod_ref, o_ref, wo_ref, bo_ref,
         l1g_ref, l1b_ref, w1_ref, b1_ref, w2_ref, b2_ref, l2g_ref, l2b_ref, out_ref) = refs

    def mix(rows):
        if even:
            ya = (hf_ref[rows, :] + hb_ref[rows, :]) * jax.nn.gelu(gate_ref[rows, :])
            return (_dot(ya.astype(BF16), woa_ref[...]) + _dot(yb_ref[rows, :], wob_ref[...])
                    + bo_ref[...])
        return _dot(o_ref[rows, :], wo_ref[...]) + bo_ref[...]

    def mlp(hm):
        acc = jnp.zeros((hm.shape[0], D_MODEL), F32)
        for c0 in range(0, D_FF, FF_CHUNK):
            t = _dot(hm, w1_ref[:, c0:c0 + FF_CHUNK]) + b1_ref[:, c0:c0 + FF_CHUNK]
            t = jnp.square(jnp.maximum(t, 0.0)).astype(BF16)
            acc = acc + _dot(t, w2_ref[c0:c0 + FF_CHUNK, :])
        return acc + b2_ref[...]

    n = x_ref.shape[0] // POST_SPLIT
    groups = [pl.ds(g * n, n) for g in range(POST_SPLIT)]
    ys = [mix(rows) for rows in groups]
    x1s = [_ln(ALPHA * x_ref[rows, :] + mod_ref[0, 2:3, :] * y) * l1g_ref[...] + l1b_ref[...]
           for rows, y in zip(groups, ys)]
    hms = [_modulate(x1, mod_ref, 3, 4).astype(BF16) for x1 in x1s]
    for rows, x1, hm in zip(groups, x1s, hms):
        out_ref[rows, :] = (_ln(ALPHA * x1 + mod_ref[0, 5:6, :] * mlp(hm)) * l2g_ref[...]
                            + l2b_ref[...])


def _post(even, x, mod, blocks_per_batch, acts, mix_weights, tail_weights):
    t = x.shape[0]
    tok = lambda n, col=0: pl.BlockSpec((TOKEN_BLOCK, n), lambda i: (i, col))
    act_specs = [tok(width, col) for _, width, col in acts]
    consts = list(mix_weights) + list(tail_weights)
    return pl.pallas_call(
        functools.partial(_post_kernel, even),
        grid=(t // TOKEN_BLOCK,),
        in_specs=[tok(D_MODEL), _mod_spec(blocks_per_batch)] + act_specs
                 + [_const_spec(w.shape) for w in consts],
        out_specs=tok(D_MODEL),
        out_shape=jax.ShapeDtypeStruct((t, D_MODEL), F32),
        compiler_params=_params("arbitrary"),
        name="post_even" if even else "post_odd",
    )(x, mod, *[a for a, _, _ in acts], *consts)


def _qkv_kernel(x_ref, mod_ref, w_ref, b_ref, q_ref, k_ref, v_ref, *cache_refs):
    h = _modulate(x_ref[...], mod_ref, 0, 1)
    u = _dot(h.astype(BF16), w_ref[...]) + b_ref[...]
    q_ref[...] = (u[:, :D_MODEL] * Q_SCALE).astype(BF16)
    k_ref[...] = u[:, D_MODEL:2 * D_MODEL].astype(BF16)
    v_ref[...] = u[:, 2 * D_MODEL:].astype(BF16)
    for which, ref in enumerate(cache_refs):
        n_seq, _, s, _ = ref.shape
        for b in range(n_seq):
            for hd in range(N_HEADS):
                c0 = (1 + which) * D_MODEL + hd * HEAD_DIM
                ref[b, hd] = u[b * s:(b + 1) * s, c0:c0 + HEAD_DIM]


def _qkv(x, mod, blocks_per_batch, w_qkv, b_qkv, cache_seq):
    t = x.shape[0]
    tok = pl.BlockSpec((TOKEN_BLOCK, D_MODEL), lambda i: (i, 0))
    out_specs = [tok, tok, tok]
    out_shape = [jax.ShapeDtypeStruct((t, D_MODEL), BF16)] * 3
    if cache_seq:
        per_block = TOKEN_BLOCK // cache_seq
        out_specs += [pl.BlockSpec((per_block, N_HEADS, cache_seq, HEAD_DIM),
                                   lambda i: (i, 0, 0, 0))] * 2
        out_shape += [jax.ShapeDtypeStruct((t // cache_seq, N_HEADS, cache_seq, HEAD_DIM), F32)] * 2
    return pl.pallas_call(
        _qkv_kernel,
        grid=(t // TOKEN_BLOCK,),
        in_specs=[tok, _mod_spec(blocks_per_batch),
                  _const_spec((D_MODEL, 3 * D_MODEL)), _const_spec((1, 3 * D_MODEL))],
        out_specs=out_specs,
        out_shape=out_shape,
        compiler_params=_params("arbitrary"),
        name="qkv",
    )(x, mod, w_qkv, b_qkv)


def _head_mask(rows):
    return lax.broadcasted_iota(jnp.int32, (rows, HEAD_PAIR), 1) >= HEAD_DIM


def _ctx_attn_kernel(q_ref, k_ref, v_ref, o_ref):
    n = q_ref.shape[1]
    second = _head_mask(n)
    for hp in range(N_PAIRS):
        lanes = slice(hp * HEAD_PAIR, (hp + 1) * HEAD_PAIR)
        q = q_ref[0, :, lanes]
        k = k_ref[0, :, lanes]
        v = v_ref[0, :, lanes]
        zero = jnp.zeros_like(q)
        q2 = jnp.concatenate([jnp.where(second, zero, q), jnp.where(second, q, zero)], axis=0)
        s = _dot_t(q2, k)
        p = jnp.exp2(s - jnp.max(s, axis=-1, keepdims=True))
        l = jnp.sum(p, axis=-1, keepdims=True)
        o2 = _dot(p.astype(BF16), v) / l
        o_ref[0, :, lanes] = jnp.where(second, o2[n:], o2[:n]).astype(o_ref.dtype)


def _ctx_attn(q, k, v):
    b, s, _ = q.shape
    spec = pl.BlockSpec((1, s, D_MODEL), lambda bi: (bi, 0, 0))
    return pl.pallas_call(
        _ctx_attn_kernel,
        grid=(b,),
        in_specs=[spec, spec, spec],
        out_specs=spec,
        out_shape=jax.ShapeDtypeStruct((b, s, D_MODEL), BF16),
        compiler_params=_params("arbitrary"),
        name="ctx_attn",
    )(q, k, v)


def _rpb_kernel(rpb_ref, out_ref):
    h = pl.program_id(0)
    shape = (GRID_W, HEAD_PAIR)
    qcol = lax.broadcasted_iota(jnp.int32, shape, 0)
    lane = lax.broadcasted_iota(jnp.int32, shape, 1)
    second = lane >= GRID_W
    kcol = jnp.where(second, lane - GRID_W, lane)
    off = jnp.clip(kcol - qcol + NA_COLS - 1, 0, 2 * NA_COLS - 2)
    start = jnp.clip(qcol - NA_COLS // 2, 0, GRID_W - NA_COLS)
    visible = jnp.logical_and(kcol >= start, kcol < start + NA_COLS)
    n_off = 2 * NA_COLS - 1
    base = h * ((2 * NA_ROWS - 1) * n_off)
    for ro in range(BIAS_ROWS):
        t = jnp.zeros(shape, F32)
        for dd in range(n_off):
            val = jnp.where(second, rpb_ref[base + (ro + 1) * n_off + dd],
                            rpb_ref[base + ro * n_off + dd])
            t = jnp.where(off == dd, val, t)
        out_ref[0, ro] = jnp.where(visible, t * LOG2E, -jnp.inf)


def _rpb_tables(rpb):
    return pl.pallas_call(
        _rpb_kernel,
        grid=(N_HEADS,),
        in_specs=[pl.BlockSpec(memory_space=pltpu.SMEM)],
        out_specs=pl.BlockSpec((1, BIAS_ROWS, GRID_W, HEAD_PAIR), lambda h: (h, 0, 0, 0)),
        out_shape=jax.ShapeDtypeStruct((N_HEADS, BIAS_ROWS, GRID_W, HEAD_PAIR), F32),
        compiler_params=_params("arbitrary"),
        name="rpb_tables",
    )(rpb.reshape(-1))


def _lat_attn_kernel(q_ref, k_ref, v_ref, kc_in_ref, vc_in_ref, bias_ref, o_ref, s_ref, kc_ref,
                     vc_ref):
    rows = q_ref.shape[1] // GRID_W
    window = NA_ROWS * GRID_W
    second = _head_mask(GRID_W)
    kc_ref[...] = jnp.concatenate([kc_in_ref[0, 0], kc_in_ref[0, 1]], axis=-1).astype(BF16)
    vc_ref[...] = jnp.concatenate([vc_in_ref[0, 0], vc_in_ref[0, 1]], axis=-1).astype(BF16)

    def window_start(i):
        rs = jnp.clip(i - NA_ROWS // 2, 0, rows - NA_ROWS)
        return rs, pl.multiple_of(rs * GRID_W, GRID_W)

    def logits(i, slot):
        rs, k0 = window_start(i)
        ro = rs - i + NA_ROWS - 1
        q = q_ref[0, pl.ds(pl.multiple_of(i * GRID_W, GRID_W), GRID_W), :]
        zero = jnp.zeros_like(q)
        q2 = jnp.concatenate([jnp.where(second, zero, q), jnp.where(second, q, zero)], axis=0)
        bias = jnp.concatenate(
            [jnp.concatenate([bias_ref[0, ro + 2 * a], bias_ref[1, ro + 2 * a]], axis=0)
             for a in range(NA_ROWS // 2)], axis=1)
        s_ref[slot, :, :window] = _dot_t(q2, k_ref[0, pl.ds(k0, window), :]) + bias
        s_ref[slot, :, window:] = _dot_t(q2, kc_ref[...])

    def finish(i, slot):
        _, k0 = window_start(i)
        s = s_ref[slot]
        p = jnp.exp2(s - jnp.max(s, axis=-1, keepdims=True))
        l = jnp.sum(p, axis=-1, keepdims=True)
        p = p.astype(BF16)
        o2 = (_dot(p[:, :window], v_ref[0, pl.ds(k0, window), :])
              + _dot(p[:, window:], vc_ref[...])) / l
        out = jnp.where(second, o2[GRID_W:], o2[:GRID_W])
        o_ref[0, pl.ds(pl.multiple_of(i * GRID_W, GRID_W), GRID_W), :] = out.astype(o_ref.dtype)

    logits(0, 0)

    def body(j, carry):
        i = 2 * j
        logits(i + 1, 1)
        finish(i, 0)
        logits(jnp.minimum(i + 2, rows - 1), 0)
        finish(i + 1, 1)
        return carry

    lax.fori_loop(0, rows // 2, body, 0, unroll=8)


def _lat_attn(q, k, v, kc, vc, bias):
    b, n, _ = q.shape
    lc = kc.shape[2]
    seq = pl.BlockSpec((1, n, HEAD_PAIR), lambda bi, hp: (bi, 0, hp))
    ctx = pl.BlockSpec((1, 2, lc, HEAD_DIM), lambda bi, hp: (bi, hp, 0, 0))
    return pl.pallas_call(
        _lat_attn_kernel,
        grid=(b, N_PAIRS),
        in_specs=[seq, seq, seq, ctx, ctx,
                  pl.BlockSpec((2, BIAS_ROWS, GRID_W, HEAD_PAIR), lambda bi, hp: (hp, 0, 0, 0))],
        out_specs=seq,
        out_shape=jax.ShapeDtypeStruct((b, n, D_MODEL), BF16),
        scratch_shapes=[pltpu.VMEM((2, 2 * GRID_W, NA_ROWS * GRID_W + lc), F32),
                        pltpu.VMEM((lc, HEAD_PAIR), BF16), pltpu.VMEM((lc, HEAD_PAIR), BF16)],
        compiler_params=_params("arbitrary", "arbitrary"),
        name="lat_attn",
    )(q, k, v, kc, vc, bias)


def _angles(rows, cols, period):
    theta = 2.0 * np.pi * ((rows[:, None] * cols[None, :]) % period) / period
    return np.cos(theta), np.sin(theta)


def _channel_dft():
    idx = np.arange(FOURIER_CH)
    c, s = _angles(idx, idx, FOURIER_CH)
    eye = np.eye(FOURIER_GROUPS)
    scale = FOURIER_CH ** -0.5
    return jnp.asarray(np.kron(eye, c) * scale, F32), jnp.asarray(np.kron(eye, s) * scale, F32)


def _position_tables(n, q):
    pos = np.arange(n)
    cb, sb = _angles(np.arange(q), pos, n)
    scale = n ** -0.5
    b_tables = [jnp.asarray(cb * scale, F32), jnp.asarray(-sb * scale, F32)]
    if q == n:
        return b_tables
    ca, sa = _angles(np.arange(n // q) * q, pos, n)
    return [jnp.asarray(ca, F32), jnp.asarray(-sa, F32)] + b_tables


def _block_diag(w):
    eye = jnp.eye(RNN_HEADS, dtype=w.dtype)
    return jnp.einsum('hij,hk->hikj', w, eye).reshape(D_RNN, D_RNN)


def kernel(x_prompt, x_sample, c, state_lru, cache_k, cache_v, c_ctx, ada_w, ada_b, ln1_g, ln1_b, ln2_g, ln2_b, w1, b1, w2, b2, e_w_in, e_b_in, e_conv_w, e_conv_b, e_w_r, e_b_r, e_w_i, e_b_i, e_lam, e_w_out, e_b_out, o_w_qkv, o_b_qkv, o_rpb, o_w_out, o_b_out):
    bp, sp, _ = x_prompt.shape
    bs, ss, _ = x_sample.shape
    row = lambda v: v.reshape(1, -1)

    cond = jnp.concatenate(
        [c_ctx[None], c, jnp.zeros((MOD_ROWS - 1 - bs, D_MODEL), F32)], axis=0)
    mod = _ada_params(cond, ada_w, ada_b)

    streams = [
        dict(x=x_prompt.reshape(bp * sp, D_MODEL), b=bp, s=sp, bpb=0, rows=slice(0, 1)),
        dict(x=x_sample.reshape(bs * ss, D_MODEL), b=bs, s=ss, bpb=ss // TOKEN_BLOCK,
             rows=slice(1, 1 + bs)),
    ]
    cc, sc = _channel_dft()
    new_lru, new_k, new_v = [], [], []

    for layer in range(DEPTH):
        j = layer // 2
        tail = [row(ln1_g[layer]), row(ln1_b[layer]), w1[layer].astype(BF16), row(b1[layer]),
                w2[layer].astype(BF16), row(b2[layer]), row(ln2_g[layer]), row(ln2_b[layer])]
        if layer % 2 == 0:
            w_in = e_w_in[j].astype(BF16)
            w_out = e_w_out[j].astype(BF16)
            wg = jnp.stack([jnp.concatenate([_block_diag(e_w_r[j, d]), _block_diag(e_w_i[j, d])],
                                            axis=1) for d in range(2)]).astype(BF16)
            bg = jnp.concatenate([e_b_r[j], e_b_i[j]], axis=-1)[:, None, :]
            lam = e_lam[j][:, None, :]
            mix = [w_out[:D_RNN], w_out[D_RNN:], row(e_b_out[j])]
            for st in streams:
                m = mod[layer, st['rows']]
                b, s = st['b'], st['s']
                urg, zc, zs = _even_in(st['x'], m, st['bpb'], w_in, row(e_b_in[j]), cc, sc)
                if st['bpb']:
                    h0 = state_lru[:, j]
                else:
                    h0 = jnp.zeros((b, 2, D_RNN), F32)
                hf, hb, fin = _rglru(urg.reshape(b, s, 2 * D_RNN), h0, e_conv_w[j],
                                     row(e_conv_b[j]), wg, bg, lam)
                yb = _fourier(zc.reshape(b, s, D_FOURIER), zs.reshape(b, s, D_FOURIER))
                acts = [(hf.reshape(b * s, D_RNN), D_RNN, 0), (hb.reshape(b * s, D_RNN), D_RNN, 0),
                        (urg, D_RNN, 1), (yb.reshape(b * s, D_FOURIER), D_FOURIER, 0)]
                st['x'] = _post(True, st['x'], m, st['bpb'], acts, mix, tail)
                if not st['bpb']:
                    new_lru.append(fin)
        else:
            w_qkv = o_w_qkv[j].astype(BF16)
            mix = [o_w_out[j].astype(BF16), row(o_b_out[j])]
            bias = _rpb_tables(o_rpb[j])
            for st in streams:
                m = mod[layer, st['rows']]
                b, s = st['b'], st['s']
                latent = bool(st['bpb'])
                q, k, v, *kv_heads = _qkv(st['x'], m, st['bpb'], w_qkv, row(o_b_qkv[j]),
                                          0 if latent else s)
                shape = (b, s, D_MODEL)
                if latent:
                    o = _lat_attn(q.reshape(shape), k.reshape(shape), v.reshape(shape),
                                  cache_k[:, j], cache_v[:, j], bias)
                else:
                    o = _ctx_attn(q.reshape(shape), k.reshape(shape), v.reshape(shape))
                    new_k.append(kv_heads[0])
                    new_v.append(kv_heads[1])
                st['x'] = _post(False, st['x'], m, st['bpb'], [(o.reshape(b * s, D_MODEL), D_MODEL, 0)],
                                mix, tail)

    return (streams[0]['x'].reshape(bp, sp, D_MODEL), streams[1]['x'].reshape(bs, ss, D_MODEL),
            jnp.stack(new_lru, axis=1), jnp.stack(new_k, axis=1), jnp.stack(new_v, axis=1))
```
